```python
import math, functools
import jax, jax.numpy as jnp
from jax import lax
import numpy as np

D_MODEL = 1024
BATCH = 4
SEQ = 8192
DEPTH = 1
DEC_BATCH = 128
DEC_SEQ = 1
PAST_LEN = 8192
PAGE_SIZE = 128

D_MIX = D_MODEL
ATTN_WIDTH = D_MIX // 2
DA_HEAD_DIM = 64
DA_V_DIM = 2 * DA_HEAD_DIM
N_ATTN_HEADS = ATTN_WIDTH // DA_V_DIM
SSM_WIDTH = D_MIX - ATTN_WIDTH
SSM_HEAD_DIM = 64
N_SSM_HEADS = SSM_WIDTH // SSM_HEAD_DIM
SSM_GROUPS = 2
SSM_HEADS_PER_GROUP = N_SSM_HEADS // SSM_GROUPS
SSM_STATE = 128
SSM_CONV = 4
SSM_CHUNK = 128
CONV_CH = SSM_WIDTH + 2 * SSM_GROUPS * SSM_STATE
D_FF = 2816
FFN_CONV = 3
Q_BLOCK = 128
EPS = 1e-5
IN_SPLITS = (ATTN_WIDTH, 2 * ATTN_WIDTH, 3 * ATTN_WIDTH, 3 * ATTN_WIDTH + SSM_WIDTH,
             3 * ATTN_WIDTH + SSM_WIDTH + CONV_CH)
IN_COLS = 3 * ATTN_WIDTH + SSM_WIDTH + CONV_CH + N_SSM_HEADS

kernel_name = "hymba_diffattn_ssd_convffn_step"


def rmsnorm(x, g):
    xf = x.astype(jnp.float32)
    y = xf * lax.rsqrt(jnp.mean(xf * xf, axis=-1, keepdims=True) + EPS)
    return (y * g.astype(jnp.float32)).astype(x.dtype)


def lambda_init(layer):
    return 0.8 - 0.6 * math.exp(-0.3 * layer)


def causal_dwconv(u, buf, w, b):
    width = w.shape[0]
    L = u.shape[1]
    up = jnp.concatenate([buf.astype(u.dtype), u], axis=1)
    y = b.astype(u.dtype) + w[0] * up[:, 0:L]
    for j in range(1, width):
        y = y + w[j] * up[:, j:j + L]
    return y, up[:, L:]


def da_scores(q, k):
    return jnp.einsum('bqhcd,bkhcd->bhcqk', q, k,
                      preferred_element_type=jnp.float32) * (DA_HEAD_DIM ** -0.5)


def diff_weights(s, lam):
    p = jax.nn.softmax(s, axis=-1)
    return p[:, :, 0] - lam * p[:, :, 1]


def attn_prompt(q, k, v, lam):
    Bsz, S = q.shape[:2]
    nb = S // Q_BLOCK
    qb = jnp.moveaxis(q.reshape(Bsz, nb, Q_BLOCK, N_ATTN_HEADS, 2, DA_HEAD_DIM), 1, 0)
    k_pos = jnp.arange(S)

    def block(args):
        i, qi = args
        q_pos = i * Q_BLOCK + jnp.arange(Q_BLOCK)
        s = da_scores(qi, k)
        s = jnp.where(k_pos[None, :] <= q_pos[:, None], s, -jnp.inf)
        a = diff_weights(s, lam)
        return jnp.einsum('bhqk,bkhe->bqhe', a.astype(v.dtype), v,
                          preferred_element_type=jnp.float32)

    o = lax.map(block, (jnp.arange(nb), qb))
    return jnp.moveaxis(o, 0, 1).reshape(Bsz, S, N_ATTN_HEADS, DA_V_DIM)


def attn_sample(q, k, v, lam, cache_k, cache_v, page_table, layer):
    DB, Lq = q.shape[:2]
    kp = cache_k[layer, page_table].reshape(DB, -1, N_ATTN_HEADS, 2, DA_HEAD_DIM)
    vp = cache_v[layer, page_table].reshape(DB, -1, N_ATTN_HEADS, DA_V_DIM)
    P = kp.shape[1]
    s_past = da_scores(q, kp.astype(q.dtype))
    s_new = da_scores(q, k)
    causal = jnp.tril(jnp.ones((Lq, Lq), dtype=bool))
    s_new = jnp.where(causal, s_new, -jnp.inf)
    a = diff_weights(jnp.concatenate([s_past, s_new], axis=-1), lam)
    o = jnp.einsum('bhqk,bkhe->bqhe', a[..., :P].astype(v.dtype), vp.astype(v.dtype),
                   preferred_element_type=jnp.float32)
    o = o + jnp.einsum('bhqk,bkhe->bqhe', a[..., P:].astype(v.dtype), v,
                       preferred_element_type=jnp.float32)
    return o


def ssd_chunk(state, x, dt, Bm, Cm, A):
    L = x.shape[1]
    xf = x.astype(jnp.float32)
    Bf = Bm.astype(jnp.float32)
    Cf = Cm.astype(jnp.float32)
    cs = jnp.cumsum(dt * A, axis=1)
    seg = cs[:, :, None] - cs[:, None, :]
    causal = jnp.tril(jnp.ones((L, L), dtype=bool))[None, :, :, None, None]
    decay = jnp.exp(jnp.where(causal, seg, -jnp.inf))
    cb = jnp.einsum('btgn,bsgn->btsg', Cf, Bf)
    w = cb[..., None] * decay * dt[:, None]
    y = jnp.einsum('btsgr,bsgrp->btgrp', w, xf)
    y = y + jnp.exp(cs)[..., None] * jnp.einsum('btgn,bgrpn->btgrp', Cf, state)
    last = cs[:, -1]
    wB = jnp.exp(last[:, None] - cs) * dt
    new_state = (jnp.exp(last)[..., None, None] * state
                 + jnp.einsum('bsgr,bsgrp,bsgn->bgrpn', wB, xf, Bf))
    return new_state, y


def ssd_scan(x, dt, Bm, Cm, A, state0, chunk):
    Bsz, L = x.shape[:2]
    nc = L // chunk

    def to_chunks(t):
        return jnp.moveaxis(t.reshape(Bsz, nc, chunk, *t.shape[2:]), 1, 0)

    def step(st, xs):
        return ssd_chunk(st, *xs, A)

    st, y = lax.scan(step, state0, (to_chunks(x), to_chunks(dt), to_chunks(Bm), to_chunks(Cm)))
    y = jnp.moveaxis(y, 0, 1).reshape(Bsz, L, N_SSM_HEADS * SSM_HEAD_DIM)
    return st, y


def run_layer(x, attend, ssm_state0, ssm_buf, ffn_buf, chunk, layer, p):
    Bsz, L = x.shape[:2]
    h = rmsnorm(x, p['g_mix'])
    u = h @ p['w_in']
    q, k, v, z, xbc, dt_raw = jnp.split(u, IN_SPLITS, axis=-1)
    q = q.reshape(Bsz, L, N_ATTN_HEADS, 2, DA_HEAD_DIM)
    k = k.reshape(Bsz, L, N_ATTN_HEADS, 2, DA_HEAD_DIM)
    v = v.reshape(Bsz, L, N_ATTN_HEADS, DA_V_DIM)
    lam0 = lambda_init(layer)
    f32 = jnp.float32
    lam = (jnp.exp(jnp.sum(p['lam_q1'].astype(f32) * p['lam_k1'].astype(f32)))
           - jnp.exp(jnp.sum(p['lam_q2'].astype(f32) * p['lam_k2'].astype(f32))) + lam0)
    o = attend(q, k, v, lam)
    o = rmsnorm(o, p['g_subln']) * (1.0 - lam0)
    o_attn = o.reshape(Bsz, L, ATTN_WIDTH).astype(x.dtype)
    xbc_c, ssm_buf_new = causal_dwconv(xbc, ssm_buf, p['ssm_conv_w'], p['ssm_conv_b'])
    xbc_c = jax.nn.silu(xbc_c)
    xs, Bm, Cm = jnp.split(xbc_c, (SSM_WIDTH, SSM_WIDTH + SSM_GROUPS * SSM_STATE), axis=-1)
    xs = xs.reshape(Bsz, L, SSM_GROUPS, SSM_HEADS_PER_GROUP, SSM_HEAD_DIM)
    Bm = Bm.reshape(Bsz, L, SSM_GROUPS, SSM_STATE)
    Cm = Cm.reshape(Bsz, L, SSM_GROUPS, SSM_STATE)
    dt = jax.nn.softplus((dt_raw + p['dt_bias']).astype(f32)).reshape(
        Bsz, L, SSM_GROUPS, SSM_HEADS_PER_GROUP)
    A = -jnp.exp(p['A_log'].astype(f32)).reshape(SSM_GROUPS, SSM_HEADS_PER_GROUP)
    st0 = ssm_state0.astype(f32).reshape(Bsz, SSM_GROUPS, SSM_HEADS_PER_GROUP, SSM_HEAD_DIM, SSM_STATE)
    st, y = ssd_scan(xs, dt, Bm, Cm, A, st0, chunk)
    Dk = p['D_skip'].astype(f32)
    y = y + (Dk[:, None] * xs.astype(f32).reshape(Bsz, L, N_SSM_HEADS, SSM_HEAD_DIM)).reshape(Bsz, L, SSM_WIDTH)
    yg = y * jax.nn.silu(z.astype(f32))
    gs = SSM_WIDTH // SSM_GROUPS
    yg = rmsnorm(yg.reshape(Bsz, L, SSM_GROUPS, gs), p['g_ssm'].reshape(SSM_GROUPS, gs))
    o_ssm = yg.reshape(Bsz, L, SSM_WIDTH).astype(x.dtype)
    x = x + jnp.concatenate([o_attn, o_ssm], axis=-1) @ p['w_out']
    h2 = rmsnorm(x, p['g_ffn'])
    up = h2 @ p['w_up']
    upc, ffn_buf_new = causal_dwconv(up, ffn_buf, p['ffn_conv_w'], p['ffn_conv_b'])
    g, val = jnp.split(upc, 2, axis=-1)
    x = x + (jax.nn.silu(g) * val) @ p['w_down']
    k_rows = k.reshape(Bsz, L, N_ATTN_HEADS, 2 * DA_HEAD_DIM)
    st_out = st.reshape(Bsz, N_SSM_HEADS, SSM_HEAD_DIM, SSM_STATE)
    return x, k_rows, v, st_out, ssm_buf_new, ffn_buf_new


def setup_inputs(seed: int = 0) -> dict:
    key = jax.random.key(seed)
    ks = jax.random.split(key, 32)
    n_pages = PAST_LEN // PAGE_SIZE
    n_used = DEC_BATCH * n_pages
    n_phys = n_used + (n_used + 3) // 4
    nrm = jax.random.normal
    f32 = jnp.float32
    page_table = jax.random.permutation(ks[0], n_phys)[:n_used].reshape(DEC_BATCH, n_pages).astype(jnp.int32)
    dt0 = jnp.exp(jax.random.uniform(ks[1], (DEPTH, N_SSM_HEADS)) * (math.log(0.1) - math.log(0.001)) + math.log(0.001))
    return {
        "x_prompt": nrm(ks[2], (BATCH, SEQ, D_MODEL), f32),
        "x_sample": nrm(ks[3], (DEC_BATCH, DEC_SEQ, D_MODEL), f32),
        "cache_k": nrm(ks[4], (DEPTH, n_phys, PAGE_SIZE, N_ATTN_HEADS, 2 * DA_HEAD_DIM), f32),
        "cache_v": nrm(ks[5], (DEPTH, n_phys, PAGE_SIZE, N_ATTN_HEADS, DA_V_DIM), f32),
        "page_table": page_table,
        "state_ssm": 0.1 * nrm(ks[6], (DEPTH, DEC_BATCH, N_SSM_HEADS, SSM_HEAD_DIM, SSM_STATE), f32),
        "state_ssm_conv": nrm(ks[7], (DEPTH, DEC_BATCH, SSM_CONV - 1, CONV_CH), f32),
        "state_ffn_conv": nrm(ks[8], (DEPTH, DEC_BATCH, FFN_CONV - 1, 2 * D_FF), f32),
        "g_mix": 1.0 + 0.02 * nrm(ks[9], (DEPTH, D_MODEL), f32),
        "w_in": nrm(ks[10], (DEPTH, D_MODEL, IN_COLS), f32) * D_MODEL ** -0.5,
        "lam_q1": 0.1 * nrm(ks[11], (DEPTH, DA_HEAD_DIM), f32),
        "lam_k1": 0.1 * nrm(ks[12], (DEPTH, DA_HEAD_DIM), f32),
        "lam_q2": 0.1 * nrm(ks[13], (DEPTH, DA_HEAD_DIM), f32),
        "lam_k2": 0.1 * nrm(ks[14], (DEPTH, DA_HEAD_DIM), f32),
        "g_subln": 1.0 + 0.02 * nrm(ks[15], (DEPTH, DA_V_DIM), f32),
        "ssm_conv_w": nrm(ks[16], (DEPTH, SSM_CONV, CONV_CH), f32) * SSM_CONV ** -0.5,
        "ssm_conv_b": 0.02 * nrm(ks[17], (DEPTH, CONV_CH), f32),
        "dt_bias": dt0 + jnp.log(-jnp.expm1(-dt0)),
        "A_log": jnp.log(jax.random.uniform(ks[18], (DEPTH, N_SSM_HEADS), f32, 1.0, 16.0)),
        "D_skip": 1.0 + 0.02 * nrm(ks[19], (DEPTH, N_SSM_HEADS), f32),
        "g_ssm": 1.0 + 0.02 * nrm(ks[20], (DEPTH, SSM_WIDTH), f32),
        "w_out": nrm(ks[21], (DEPTH, D_MIX, D_MODEL), f32) * D_MIX ** -0.5,
        "g_ffn": 1.0 + 0.02 * nrm(ks[22], (DEPTH, D_MODEL), f32),
        "w_up": nrm(ks[23], (DEPTH, D_MODEL, 2 * D_FF), f32) * D_MODEL ** -0.5,
        "ffn_conv_w": nrm(ks[24], (DEPTH, FFN_CONV, 2 * D_FF), f32) * FFN_CONV ** -0.5,
        "ffn_conv_b": 0.02 * nrm(ks[25], (DEPTH, 2 * D_FF), f32),
        "w_down": nrm(ks[26], (DEPTH, D_FF, D_MODEL), f32) * D_FF ** -0.5,
        "g_final": 1.0 + 0.02 * nrm(ks[27], (D_MODEL,), f32),
    }


def reference(x_prompt, x_sample, cache_k, cache_v, page_table, state_ssm, state_ssm_conv,
              state_ffn_conv, g_mix, w_in, lam_q1, lam_k1, lam_q2, lam_k2, g_subln,
              ssm_conv_w, ssm_conv_b, dt_bias, A_log, D_skip, g_ssm, w_out, g_ffn, w_up,
              ffn_conv_w, ffn_conv_b, w_down, g_final):
    xp, xs = x_prompt, x_sample
    Bp = x_prompt.shape[0]
    kp_l, vp_l, sp_l, cp_l, fp_l = [], [], [], [], []
    ks_l, vs_l, ss_l, cs_l, fs_l = [], [], [], [], []
    for l in range(DEPTH):
        p = dict(g_mix=g_mix[l], w_in=w_in[l], lam_q1=lam_q1[l], lam_k1=lam_k1[l],
                 lam_q2=lam_q2[l], lam_k2=lam_k2[l], g_subln=g_subln[l],
                 ssm_conv_w=ssm_conv_w[l], ssm_conv_b=ssm_conv_b[l], dt_bias=dt_bias[l],
                 A_log=A_log[l], D_skip=D_skip[l], g_ssm=g_ssm[l], w_out=w_out[l],
                 g_ffn=g_ffn[l], w_up=w_up[l], ffn_conv_w=ffn_conv_w[l],
                 ffn_conv_b=ffn_conv_b[l], w_down=w_down[l])
        zs = jnp.zeros((Bp, N_SSM_HEADS, SSM_HEAD_DIM, SSM_STATE), jnp.float32)
        zc = jnp.zeros((Bp, SSM_CONV - 1, CONV_CH), xp.dtype)
        zf = jnp.zeros((Bp, FFN_CONV - 1, 2 * D_FF), xp.dtype)
        xp, k1, v1, s1, c1, f1 = run_layer(xp, attn_prompt, zs, zc, zf, SSM_CHUNK, l, p)
        kp_l.append(k1); vp_l.append(v1); sp_l.append(s1); cp_l.append(c1); fp_l.append(f1)
        att_s = functools.partial(attn_sample, cache_k=cache_k, cache_v=cache_v,
                                  page_table=page_table, layer=l)
        xs, k2, v2, s2, c2, f2 = run_layer(xs, att_s, state_ssm[l], state_ssm_conv[l],
                                           state_ffn_conv[l], xs.shape[1], l, p)
        ks_l.append(k2); vs_l.append(v2); ss_l.append(s2); cs_l.append(c2); fs_l.append(f2)
    y_prompt = rmsnorm(xp, g_final)
    y_sample = rmsnorm(xs, g_final)
    return (y_prompt, y_sample,
            jnp.stack(kp_l), jnp.stack(vp_l), jnp.stack(sp_l), jnp.stack(cp_l), jnp.stack(fp_l),
            jnp.stack(ks_l), jnp.stack(vs_l), jnp.stack(ss_l), jnp.stack(cs_l), jnp.stack(fs_l))
```

```python
import functools
import math

import jax
import jax.numpy as jnp
from jax import lax
from jax.experimental import pallas as pl
from jax.experimental.pallas import tpu as pltpu

F32 = jnp.float32
BF16 = jnp.bfloat16

EPS = 1e-5
D_MODEL = 1024
ATTN_WIDTH = 512
DA_HEAD_DIM = 64
DA_V_DIM = 128
N_ATTN_HEADS = 4
SSM_WIDTH = 512
SSM_HEAD_DIM = 64
N_SSM_HEADS = 8
SSM_GROUPS = 2
SSM_HEADS_PER_GROUP = 4
SSM_STATE = 128
SSM_CONV = 4
SSM_CHUNK = 128
CONV_CH = 1024
D_FF = 2816
FFN_CONV = 3
PAGE_SIZE = 128
LANES = 128
DT_PAD = LANES
IN_COLS_PAD = 3 * ATTN_WIDTH + SSM_WIDTH + CONV_CH + DT_PAD
QK_SCALE = DA_HEAD_DIM ** -0.5
LAM0 = 0.8 - 0.6 * math.exp(-0.3 * 0)
VMEM_LIMIT = 56 * 1024 * 1024

NT_DIMS = (((1,), (1,)), ((), ()))
TN_DIMS = (((0,), (0,)), ((), ()))


def _cparams(sem):
    return pltpu.CompilerParams(dimension_semantics=sem, vmem_limit_bytes=VMEM_LIMIT)


def _const_spec(shape):
    nd = len(shape)
    return pl.BlockSpec(shape, lambda *_: (0,) * nd, pipeline_mode=pl.Buffered(1))


def _sigmoid(x):
    return 1.0 / (1.0 + jnp.exp(-x))


def _softplus(x):
    return jnp.maximum(x, 0.0) + jnp.log1p(jnp.exp(-jnp.abs(x)))


def _lam_from_params(lam_ref):
    lp = lam_ref[...]
    t1 = jnp.sum(lp[0:1] * lp[1:2], axis=-1, keepdims=True)
    t2 = jnp.sum(lp[2:3] * lp[3:4], axis=-1, keepdims=True)
    return jnp.exp(t1) - jnp.exp(t2) + LAM0


def _inproj_body(x_ref, g_ref, w_ref, q_ref, k_ref, v_ref, kb_ref, vb_ref, z_ref, xbc_ref, dt_ref):
    x = x_ref[...]
    r = lax.rsqrt(jnp.mean(x * x, axis=-1, keepdims=True) + EPS)
    h = (x * r * g_ref[...]).astype(BF16)

    def proj(lo, hi):
        return jnp.dot(h, w_ref[:, lo:hi], preferred_element_type=F32)

    a = ATTN_WIDTH
    q_ref[...] = (proj(0, a) * QK_SCALE).astype(BF16)
    k = proj(a, 2 * a)
    k_ref[...] = k
    kb_ref[...] = k.astype(BF16)
    v = proj(2 * a, 3 * a)
    v_ref[...] = v
    vb_ref[...] = v.astype(BF16)
    z_ref[...] = proj(3 * a, 3 * a + SSM_WIDTH).astype(BF16)
    c0 = 3 * a + SSM_WIDTH
    xbc_ref[...] = proj(c0, c0 + CONV_CH)
    dt_ref[...] = proj(c0 + CONV_CH, c0 + CONV_CH + DT_PAD)


def _in_proj(x2d, g_mix, w_in_b, tm):
    n = x2d.shape[0]
    a = ATTN_WIDTH
    row = lambda w: pl.BlockSpec((tm, w), lambda i: (i, 0))
    out_shape = (
        jax.ShapeDtypeStruct((n, a), BF16),
        jax.ShapeDtypeStruct((n, a), F32),
        jax.ShapeDtypeStruct((n, a), F32),
        jax.ShapeDtypeStruct((n, a), BF16),
        jax.ShapeDtypeStruct((n, a), BF16),
        jax.ShapeDtypeStruct((n, SSM_WIDTH), BF16),
        jax.ShapeDtypeStruct((n, CONV_CH), F32),
        jax.ShapeDtypeStruct((n, DT_PAD), F32),
    )
    return pl.pallas_call(
        _inproj_body,
        grid=(n // tm,),
        in_specs=[row(D_MODEL), _const_spec((1, D_MODEL)), _const_spec((D_MODEL, IN_COLS_PAD))],
        out_specs=(row(a), row(a), row(a), row(a), row(a), row(SSM_WIDTH), row(CONV_CH), row(DT_PAD)),
        out_shape=out_shape,
        compiler_params=_cparams(("parallel",)),
        name="in_proj",
    )(x2d, g_mix, w_in_b)


def _attn_body(lam_ref, gsub_ref, q_ref, k_ref, v_ref, o_ref, *, t):
    i = pl.program_id(2)
    q = q_ref[...]
    lane = lax.broadcasted_iota(jnp.int32, q.shape, 1)
    zero = jnp.zeros_like(q)
    q1 = jnp.where(lane < DA_HEAD_DIM, q, zero)
    q2 = jnp.where(lane >= DA_HEAD_DIM, q, zero)

    def update(s, vt, m, l, acc):
        m_new = jnp.maximum(m, jnp.max(s, axis=-1, keepdims=True))
        alpha = jnp.exp(m - m_new)
        p = jnp.exp(s - m_new)
        l = alpha * l + jnp.sum(p, axis=-1, keepdims=True)
        acc = alpha * acc + jnp.dot(p.astype(BF16), vt, preferred_element_type=F32)
        return m_new, l, acc

    def tile(j, carry, masked):
        m1, l1, a1, m2, l2, a2 = carry
        start = pl.multiple_of(j * t, t)
        kt = k_ref[pl.ds(start, t), :]
        vt = v_ref[pl.ds(start, t), :]
        s1 = lax.dot_general(q1, kt, NT_DIMS, preferred_element_type=F32)
        s2 = lax.dot_general(q2, kt, NT_DIMS, preferred_element_type=F32)
        if masked:
            rr = lax.broadcasted_iota(jnp.int32, (t, t), 0)
            cc = lax.broadcasted_iota(jnp.int32, (t, t), 1)
            keep = cc <= rr
            s1 = jnp.where(keep, s1, -jnp.inf)
            s2 = jnp.where(keep, s2, -jnp.inf)
        m1, l1, a1 = update(s1, vt, m1, l1, a1)
        m2, l2, a2 = update(s2, vt, m2, l2, a2)
        return m1, l1, a1, m2, l2, a2

    ninf = jnp.full((t, 1), -jnp.inf, F32)
    zl = jnp.zeros((t, 1), F32)
    za = jnp.zeros((t, DA_V_DIM), F32)
    carry = (ninf, zl, za, ninf, zl, za)
    carry = lax.fori_loop(0, i, lambda j, c: tile(j, c, False), carry)
    m1, l1, a1, m2, l2, a2 = tile(i, carry, True)

    lam = _lam_from_params(lam_ref)
    o = a1 / l1 - lam * (a2 / l2)
    r = lax.rsqrt(jnp.mean(o * o, axis=-1, keepdims=True) + EPS)
    o = (o * r * gsub_ref[...]) * (1.0 - LAM0)
    o_ref[...] = o.astype(o_ref.dtype)


def _attn_prompt(q, kb, vb, lam4, g_subln, bsz, seq, t):
    nq = seq // t
    hb = pl.BlockSpec((t, DA_V_DIM), lambda b, h, i: (b * nq + i, h))
    kv = pl.BlockSpec((seq, DA_V_DIM), lambda b, h, i: (b, h))
    return pl.pallas_call(
        functools.partial(_attn_body, t=t),
        grid=(bsz, N_ATTN_HEADS, nq),
        in_specs=[_const_spec((4, DA_HEAD_DIM)), _const_spec((1, DA_V_DIM)), hb, kv, kv],
        out_specs=hb,
        out_shape=jax.ShapeDtypeStruct((bsz * seq, ATTN_WIDTH), BF16),
        compiler_params=_cparams(("parallel", "parallel", "arbitrary")),
        name="attn_prompt",
    )(lam4, g_subln, q, kb, vb)


PAGED_ROWS = 16


def _paged_body(pt_ref, lam_ref, gsub_ref, q_ref, kn_ref, vn_ref, *refs, pg):
    del pt_ref
    k_refs = refs[:pg]
    v_refs = refs[pg:2 * pg]
    o_ref = refs[2 * pg]
    m_scr, l_scr, acc_scr = refs[2 * pg + 1:]
    g = pl.program_id(1)
    rows = PAGED_ROWS

    row = lax.broadcasted_iota(jnp.int32, (rows, ATTN_WIDTH), 0)
    lane = lax.broadcasted_iota(jnp.int32, (rows, ATTN_WIDTH), 1)
    qf = q_ref[0].astype(F32)
    qmat = jnp.where(lane // DA_HEAD_DIM == row, jnp.broadcast_to(qf, (rows, ATTN_WIDTH)), 0.0)
    qmat_b = qmat.astype(BF16)

    @pl.when(g == 0)
    def _():
        m_scr[...] = jnp.full(m_scr.shape, -jnp.inf, F32)
        l_scr[...] = jnp.zeros(l_scr.shape, F32)
        acc_scr[...] = jnp.zeros(acc_scr.shape, F32)

    s_list = []
    for kr in k_refs:
        kp = kr[...].astype(BF16)
        s_list.append(lax.dot_general(qmat_b, kp, NT_DIMS, preferred_element_type=F32))
    s_max = s_list[0]
    for s in s_list[1:]:
        s_max = jnp.maximum(s_max, s)
    m_old = m_scr[...]
    m_new = jnp.maximum(m_old, jnp.max(s_max, axis=-1, keepdims=True))
    alpha = jnp.exp(m_old - m_new)
    l_add = jnp.zeros((rows, PAGE_SIZE), F32)
    pv = jnp.zeros((rows, ATTN_WIDTH), F32)
    for s, vr in zip(s_list, v_refs):
        p = jnp.exp(s - m_new)
        l_add = l_add + p
        pv = pv + jnp.dot(p.astype(BF16), vr[...].astype(BF16), preferred_element_type=F32)
    l_new = alpha * l_scr[...] + jnp.sum(l_add, axis=-1, keepdims=True)
    acc_new = alpha * acc_scr[...] + pv
    m_scr[...] = m_new
    l_scr[...] = l_new
    acc_scr[...] = acc_new

    @pl.when(g == pl.num_programs(1) - 1)
    def _():
        s_self = jnp.sum(qmat * kn_ref[0], axis=-1, keepdims=True)
        m_fin = jnp.maximum(m_new, s_self)
        a_fin = jnp.exp(m_new - m_fin)
        p_self = jnp.exp(s_self - m_fin)
        l_fin = a_fin * l_new + p_self
        acc = a_fin * acc_new + p_self * vn_ref[0]
        lam = _lam_from_params(lam_ref)
        coef = jnp.where(row % 2 == 0, 1.0, -lam) / l_fin
        own = lane // DA_V_DIM == row // 2
        o = jnp.sum(jnp.where(own, acc * coef, 0.0), axis=0, keepdims=True)
        gs = gsub_ref[...]
        for h in range(N_ATTN_HEADS):
            oh = o[:, h * DA_V_DIM:(h + 1) * DA_V_DIM]
            r = lax.rsqrt(jnp.mean(oh * oh, axis=-1, keepdims=True) + EPS)
            o_ref[0, :, h * DA_V_DIM:(h + 1) * DA_V_DIM] = ((oh * r * gs) * (1.0 - LAM0)).astype(o_ref.dtype)


def _attn_paged(page_table, lam4, g_subln, q, k_new, v_new, cache_k, cache_v, pg):
    db, n_pages = page_table.shape
    n_groups = n_pages // pg
    ck = cache_k.reshape(-1, PAGE_SIZE, ATTN_WIDTH)
    cv = cache_v.reshape(-1, PAGE_SIZE, ATTN_WIDTH)
    tok = pl.BlockSpec((1, 1, ATTN_WIDTH), lambda b, g, pt: (b, 0, 0))

    def page_spec(i):
        return pl.BlockSpec((None, PAGE_SIZE, ATTN_WIDTH), lambda b, g, pt: (pt[b, g * pg + i], 0, 0))

    grid_spec = pltpu.PrefetchScalarGridSpec(
        num_scalar_prefetch=1,
        grid=(db, n_groups),
        in_specs=[pl.BlockSpec((4, DA_HEAD_DIM), lambda b, g, pt: (0, 0)),
                  pl.BlockSpec((1, DA_V_DIM), lambda b, g, pt: (0, 0)),
                  tok, tok, tok]
                 + [page_spec(i) for i in range(pg)] + [page_spec(i) for i in range(pg)],
        out_specs=tok,
        scratch_shapes=[pltpu.VMEM((PAGED_ROWS, 1), F32), pltpu.VMEM((PAGED_ROWS, 1), F32),
                        pltpu.VMEM((PAGED_ROWS, ATTN_WIDTH), F32)],
    )
    out = pl.pallas_call(
        functools.partial(_paged_body, pg=pg),
        grid_spec=grid_spec,
        out_shape=jax.ShapeDtypeStruct((db, 1, ATTN_WIDTH), BF16),
        compiler_params=_cparams(("parallel", "arbitrary")),
        name="attn_paged",
    )(page_table, lam4, g_subln, q.reshape(db, 1, ATTN_WIDTH), k_new.reshape(db, 1, ATTN_WIDTH),
      v_new.reshape(db, 1, ATTN_WIDTH), *([ck] * pg), *([cv] * pg))
    return out.reshape(db, ATTN_WIDTH)


def _gate_norm(y, xs, z, dskip, gssm):
    y = y + dskip * xs
    zf = z.astype(F32)
    yg = y * (zf * _sigmoid(zf))
    gw = SSM_WIDTH // SSM_GROUPS
    outs = []
    for g in range(SSM_GROUPS):
        part = yg[:, g * gw:(g + 1) * gw]
        r = lax.rsqrt(jnp.mean(part * part, axis=-1, keepdims=True) + EPS)
        outs.append(part * r * gssm[:, g * gw:(g + 1) * gw])
    return jnp.concatenate(outs, axis=-1)


def _ssd_body(xbc_ref, z_ref, dt_ref, cw_ref, cb_ref, dtb_ref, alog_ref, dskip_ref, gssm_ref,
              o_ref, st_ref, ext_ref, *, lb):
    j = pl.program_id(1)
    L = SSM_CHUNK
    hpg = SSM_HEADS_PER_GROUP
    gw = hpg * SSM_HEAD_DIM
    pad = 8

    @pl.when(j == 0)
    def _():
        ext_ref[0:pad, :] = jnp.zeros((pad, CONV_CH), F32)
        st_ref[...] = jnp.zeros(st_ref.shape, F32)

    ext_ref[pad:pad + lb, :] = xbc_ref[...]

    cw = cw_ref[...]
    cbias = cb_ref[...]
    a_neg = -jnp.exp(alog_ref[...])
    dtb = dtb_ref[...]
    dskip = dskip_ref[...]
    gssm = gssm_ref[...]
    rr = lax.broadcasted_iota(jnp.int32, (L, L), 0)
    cc = lax.broadcasted_iota(jnp.int32, (L, L), 1)
    causal = cc <= rr
    tri = causal.astype(F32)
    lane_g = lax.broadcasted_iota(jnp.int32, (L, gw), 1) // SSM_HEAD_DIM

    for c in range(lb // L):
        r0 = c * L
        xc = cbias + cw[3:4] * ext_ref[pad + r0:pad + r0 + L, :]
        for k in range(1, SSM_CONV):
            xc = xc + cw[3 - k:4 - k] * ext_ref[pad + r0 - k:pad + r0 - k + L, :]
        xc = xc * _sigmoid(xc)
        xs = xc[:, :SSM_WIDTH]
        dt = _softplus(dt_ref[r0:r0 + L, :] + dtb)
        a = dt * a_neg
        cs = jnp.dot(tri, a, preferred_element_type=F32, precision=lax.Precision.HIGHEST)
        cs_t = cs.T
        dt_t = dt.T
        ecs = jnp.exp(cs)
        wb = jnp.exp(cs[L - 1:L, :] - cs) * dt
        e_last = jnp.exp(cs_t[:, L - 1:L])

        y_parts = []
        for g in range(SSM_GROUPS):
            bg = xc[:, SSM_WIDTH + g * SSM_STATE:SSM_WIDTH + (g + 1) * SSM_STATE].astype(BF16)
            cg = xc[:, SSM_WIDTH + (SSM_GROUPS + g) * SSM_STATE:
                    SSM_WIDTH + (SSM_GROUPS + g + 1) * SSM_STATE].astype(BF16)
            xg = xs[:, g * gw:(g + 1) * gw]
            xg_b = xg.astype(BF16)
            st_g = st_ref[g * gw:(g + 1) * gw, :]
            cbm = lax.dot_general(cg, bg, NT_DIMS, preferred_element_type=F32)
            ys = lax.dot_general(cg, st_g.astype(BF16), NT_DIMS, preferred_element_type=F32)
            yg = jnp.zeros((L, gw), F32)
            ecs_g = jnp.zeros((L, gw), F32)
            wb_g = jnp.zeros((L, gw), F32)
            scale_rows = []
            for r in range(hpg):
                h = g * hpg + r
                seg = cs[:, h:h + 1] - cs_t[h:h + 1, :]
                decay = jnp.exp(jnp.where(causal, seg, -jnp.inf))
                w = (cbm * decay * dt_t[h:h + 1, :]).astype(BF16)
                yh = jnp.dot(w, xg_b, preferred_element_type=F32)
                sel = lane_g == r
                yg = jnp.where(sel, yh, yg)
                ecs_g = jnp.where(sel, ecs[:, h:h + 1], ecs_g)
                wb_g = jnp.where(sel, wb[:, h:h + 1], wb_g)
                scale_rows.append(jnp.broadcast_to(e_last[h:h + 1, :], (SSM_HEAD_DIM, SSM_STATE)))
            y_parts.append(yg + ecs_g * ys)
            xw = (xg * wb_g).astype(BF16)
            upd = lax.dot_general(xw, bg, TN_DIMS, preferred_element_type=F32)
            st_ref[g * gw:(g + 1) * gw, :] = jnp.concatenate(scale_rows, axis=0) * st_g + upd
        y = jnp.concatenate(y_parts, axis=-1)
        o_ref[r0:r0 + L, :] = _gate_norm(y, xs, z_ref[r0:r0 + L, :], dskip, gssm).astype(o_ref.dtype)

    ext_ref[0:pad, :] = ext_ref[lb:lb + pad, :]


def _ssd_prompt(xbc, z, dt, cw, cb, dtb, alog, dskip, gssm, bsz, seq, lb):
    nb = seq // lb
    row = lambda w: pl.BlockSpec((lb, w), lambda b, j: (b * nb + j, 0))
    return pl.pallas_call(
        functools.partial(_ssd_body, lb=lb),
        grid=(bsz, nb),
        in_specs=[row(CONV_CH), row(SSM_WIDTH), row(DT_PAD),
                  _const_spec((SSM_CONV, CONV_CH)), _const_spec((1, CONV_CH)),
                  _const_spec((1, DT_PAD)), _const_spec((1, DT_PAD)),
                  _const_spec((1, SSM_WIDTH)), _const_spec((1, SSM_WIDTH))],
        out_specs=(row(SSM_WIDTH),
                   pl.BlockSpec((None, SSM_WIDTH, SSM_STATE), lambda b, j: (b, 0, 0))),
        out_shape=(jax.ShapeDtypeStruct((bsz * seq, SSM_WIDTH), BF16),
                   jax.ShapeDtypeStruct((bsz, SSM_WIDTH, SSM_STATE), F32)),
        scratch_shapes=[pltpu.VMEM((lb + 8, CONV_CH), F32)],
        compiler_params=_cparams(("parallel", "arbitrary")),
        name="ssd_prompt",
    )(xbc, z, dt, cw, cb, dtb, alog, dskip, gssm)


def _ssd_step_body(xbc_ref, cbuf_ref, z_ref, dt_ref, st_ref, cw_ref, cb_ref, dtb_ref, alog_ref,
                   dskip_ref, gssm_ref, o_ref, sto_ref, y_scr, *, bb):
    hpg = SSM_HEADS_PER_GROUP
    gw = hpg * SSM_HEAD_DIM
    cw = cw_ref[...]
    xc = cb_ref[...] + cw[3:4] * xbc_ref[...]
    for k in range(SSM_CONV - 1):
        xc = xc + cw[k:k + 1] * cbuf_ref[k]
    xc = xc * _sigmoid(xc)
    xs = xc[:, :SSM_WIDTH]
    dt = _softplus(dt_ref[...] + dtb_ref[...])
    da = jnp.exp(dt * (-jnp.exp(alog_ref[...])))
    zpad = jnp.zeros((LANES - bb, LANES), F32)
    xs_t = [jnp.concatenate([xs[:, k * LANES:(k + 1) * LANES], zpad], axis=0).T
            for k in range(SSM_WIDTH // LANES)]
    lane8 = lax.broadcasted_iota(jnp.int32, (8, SSM_WIDTH), 1)
    row8 = lax.broadcasted_iota(jnp.int32, (8, SSM_STATE), 0)

    for b in range(bb):
        new_rows = []
        for h in range(N_SSM_HEADS):
            g = h // hpg
            xt = xs_t[h // 2]
            xcol = xt[(h % 2) * SSM_HEAD_DIM:(h % 2 + 1) * SSM_HEAD_DIM, b:b + 1]
            brow = xc[b:b + 1, SSM_WIDTH + g * SSM_STATE:SSM_WIDTH + (g + 1) * SSM_STATE]
            old = st_ref[b, h * SSM_HEAD_DIM:(h + 1) * SSM_HEAD_DIM, :]
            new = da[b:b + 1, h:h + 1] * old + (dt[b:b + 1, h:h + 1] * xcol) * brow
            sto_ref[b, h * SSM_HEAD_DIM:(h + 1) * SSM_HEAD_DIM, :] = new
            new_rows.append(new.astype(BF16))
        new_all = jnp.concatenate(new_rows, axis=0)
        c0 = SSM_WIDTH + SSM_GROUPS * SSM_STATE
        cmat = jnp.where(row8 == 0, xc[b:b + 1, c0:c0 + SSM_STATE],
                         jnp.where(row8 == 1, xc[b:b + 1, c0 + SSM_STATE:c0 + 2 * SSM_STATE], 0.0))
        yy = lax.dot_general(cmat.astype(BF16), new_all, NT_DIMS, preferred_element_type=F32)
        y_scr[b:b + 1, :] = jnp.where(lane8[0:1] < gw, yy[0:1], yy[1:2])
    o_ref[...] = _gate_norm(y_scr[...], xs, z_ref[...], dskip_ref[...], gssm_ref[...]).astype(o_ref.dtype)


def _ssd_step(xbc, cbuf_t, z, dt, state, cw, cb, dtb, alog, dskip, gssm, bb):
    db = xbc.shape[0]
    row = lambda w: pl.BlockSpec((bb, w), lambda i: (i, 0))
    st = pl.BlockSpec((bb, SSM_WIDTH, SSM_STATE), lambda i: (i, 0, 0))
    return pl.pallas_call(
        functools.partial(_ssd_step_body, bb=bb),
        grid=(db // bb,),
        in_specs=[row(CONV_CH), pl.BlockSpec((SSM_CONV - 1, bb, CONV_CH), lambda i: (0, i, 0)),
                  row(SSM_WIDTH), row(DT_PAD), st,
                  _const_spec((SSM_CONV, CONV_CH)), _const_spec((1, CONV_CH)),
                  _const_spec((1, DT_PAD)), _const_spec((1, DT_PAD)),
                  _const_spec((1, SSM_WIDTH)), _const_spec((1, SSM_WIDTH))],
        out_specs=(row(SSM_WIDTH), st),
        out_shape=(jax.ShapeDtypeStruct((db, SSM_WIDTH), BF16),
                   jax.ShapeDtypeStruct((db, SSM_WIDTH, SSM_STATE), F32)),
        scratch_shapes=[pltpu.VMEM((bb, SSM_WIDTH), F32)],
        compiler_params=_cparams(("parallel",)),
        name="ssd_step",
    )(xbc, cbuf_t, z, dt, state, cw, cb, dtb, alog, dskip, gssm)


FFN_CHUNK = 256


def _ffn_body(*refs, tm, sample):
    if sample:
        (x_ref, oa_ref, os_ref, wo_ref, gf_ref, wu_ref, cw_ref, cb_ref, wd_ref, gfin_ref,
         p2_ref, p1_ref, y_ref, up_ref) = refs
    else:
        (x_ref, oa_ref, os_ref, wo_ref, gf_ref, wu_ref, cw_ref, cb_ref, wd_ref, gfin_ref,
         y_ref, carry_ref, extg_ref, extv_ref) = refs
        pad = 8

        @pl.when(pl.program_id(1) == 0)
        def _():
            carry_ref[...] = jnp.zeros(carry_ref.shape, F32)

    x1 = (x_ref[...]
          + jnp.dot(oa_ref[...], wo_ref[0:ATTN_WIDTH, :], preferred_element_type=F32)
          + jnp.dot(os_ref[...], wo_ref[ATTN_WIDTH:, :], preferred_element_type=F32))
    r = lax.rsqrt(jnp.mean(x1 * x1, axis=-1, keepdims=True) + EPS)
    h2 = (x1 * r * gf_ref[...]).astype(BF16)
    cw = cw_ref[...]
    cbias = cb_ref[...]

    acc = jnp.zeros_like(x1)
    for c in range(D_FF // FFN_CHUNK):
        parts = []
        for lo in (c * FFN_CHUNK, D_FF + c * FFN_CHUNK):
            hi = lo + FFN_CHUNK
            u = jnp.dot(h2, wu_ref[:, lo:hi], preferred_element_type=F32)
            if sample:
                up_ref[:, lo:hi] = u
                u1 = p1_ref[:, lo:hi]
                u2 = p2_ref[:, lo:hi]
            else:
                ext = extg_ref if lo < D_FF else extv_ref
                ext[0:pad, :] = carry_ref[:, lo:hi]
                ext[pad:pad + tm, :] = u
                u1 = ext[pad - 1:pad - 1 + tm, :]
                u2 = ext[pad - 2:pad - 2 + tm, :]
                carry_ref[:, lo:hi] = ext[tm:tm + pad, :]
            parts.append(cbias[:, lo:hi] + cw[0:1, lo:hi] * u2 + cw[1:2, lo:hi] * u1 + cw[2:3, lo:hi] * u)
        gate, val = parts
        act = ((gate * _sigmoid(gate)) * val).astype(BF16)
        acc = acc + jnp.dot(act, wd_ref[c * FFN_CHUNK:(c + 1) * FFN_CHUNK, :], preferred_element_type=F32)
    x2 = x1 + acc
    r2 = lax.rsqrt(jnp.mean(x2 * x2, axis=-1, keepdims=True) + EPS)
    y_ref[...] = x2 * r2 * gfin_ref[...]


def _ffn_weights_specs():
    return [_const_spec((D_MODEL, D_MODEL)), _const_spec((1, D_MODEL)),
            _const_spec((D_MODEL, 2 * D_FF)), _const_spec((FFN_CONV, 2 * D_FF)),
            _const_spec((1, 2 * D_FF)), _const_spec((D_FF, D_MODEL)), _const_spec((1, D_MODEL))]


def _ffn_prompt(x2d, oa, os_, weights, bsz, seq, tm):
    nt = seq // tm
    row = lambda w: pl.BlockSpec((tm, w), lambda b, i: (b * nt + i, 0))
    return pl.pallas_call(
        functools.partial(_ffn_body, tm=tm, sample=False),
        grid=(bsz, nt),
        in_specs=[row(D_MODEL), row(ATTN_WIDTH), row(SSM_WIDTH)] + _ffn_weights_specs(),
        out_specs=(row(D_MODEL), pl.BlockSpec((None, 8, 2 * D_FF), lambda b, i: (b, 0, 0))),
        out_shape=(jax.ShapeDtypeStruct((bsz * seq, D_MODEL), F32),
                   jax.ShapeDtypeStruct((bsz, 8, 2 * D_FF), F32)),
        scratch_shapes=[pltpu.VMEM((tm + 8, FFN_CHUNK), F32), pltpu.VMEM((tm + 8, FFN_CHUNK), F32)],
        compiler_params=_cparams(("parallel", "arbitrary")),
        name="ffn_prompt",
    )(x2d, oa, os_, *weights)


def _ffn_sample(x2d, oa, os_, weights, prev2, prev1):
    db = x2d.shape[0]
    full = lambda w: pl.BlockSpec((db, w), lambda i: (0, 0))
    return pl.pallas_call(
        functools.partial(_ffn_body, tm=db, sample=True),
        grid=(1,),
        in_specs=[full(D_MODEL), full(ATTN_WIDTH), full(SSM_WIDTH)] + _ffn_weights_specs()
                 + [full(2 * D_FF), full(2 * D_FF)],
        out_specs=(full(D_MODEL), full(2 * D_FF)),
        out_shape=(jax.ShapeDtypeStruct((db, D_MODEL), F32),
                   jax.ShapeDtypeStruct((db, 2 * D_FF), F32)),
        compiler_params=_cparams(("arbitrary",)),
        name="ffn_sample",
    )(x2d, oa, os_, *weights, prev2, prev1)


def _pick(pref, n):
    t = min(pref, n)
    assert n % t == 0, (pref, n)
    return t


def kernel(x_prompt, x_sample, cache_k, cache_v, page_table, state_ssm, state_ssm_conv,
           state_ffn_conv, g_mix, w_in, lam_q1, lam_k1, lam_q2, lam_k2, g_subln,
           ssm_conv_w, ssm_conv_b, dt_bias, A_log, D_skip, g_ssm, w_out, g_ffn, w_up,
           ffn_conv_w, ffn_conv_b, w_down, g_final):
    bsz, seq, _ = x_prompt.shape
    db = x_sample.shape[0]
    assert x_sample.shape[1] == 1 and w_in.shape[0] == 1

    in_cols = w_in.shape[-1]
    w_in_b = jnp.pad(w_in[0], ((0, 0), (0, IN_COLS_PAD - in_cols))).astype(BF16)
    lam4 = jnp.concatenate([lam_q1, lam_k1, lam_q2, lam_k2], axis=0)
    head_pad = lambda v: jnp.pad(v, ((0, 0), (0, DT_PAD - N_SSM_HEADS)))
    dtb, alog = head_pad(dt_bias), head_pad(A_log)
    dskip = jnp.repeat(D_skip, SSM_HEAD_DIM, axis=-1)
    ssd_params = (ssm_conv_w[0], ssm_conv_b, dtb, alog, dskip, g_ssm)
    ffn_weights = (w_out[0].astype(BF16), g_ffn, w_up[0].astype(BF16), ffn_conv_w[0], ffn_conv_b,
                   w_down[0].astype(BF16), g_final.reshape(1, D_MODEL))

    n = bsz * seq
    xp = x_prompt.reshape(n, D_MODEL)
    q, k, v, kb, vb, z, xbc, dt = _in_proj(xp, g_mix, w_in_b, _pick(512, n))
    o_attn = _attn_prompt(q, kb, vb, lam4, g_subln, bsz, seq, _pick(512, seq))
    o_ssm, st_p = _ssd_prompt(xbc, z, dt, *ssd_params, bsz, seq, _pick(512, seq))
    y_p, ffn_carry = _ffn_prompt(xp, o_attn, o_ssm, ffn_weights, bsz, seq, _pick(512, seq))

    kv_shape = (1, bsz, seq, N_ATTN_HEADS, DA_V_DIM)
    outs_p = (y_p.reshape(bsz, seq, D_MODEL), k.reshape(kv_shape), v.reshape(kv_shape),
              st_p.reshape(1, bsz, N_SSM_HEADS, SSM_HEAD_DIM, SSM_STATE),
              xbc.reshape(bsz, seq, CONV_CH)[None, :, seq - (SSM_CONV - 1):],
              ffn_carry[None, :, 8 - (FFN_CONV - 1):])

    xs = x_sample.reshape(db, D_MODEL)
    q_s, k_s, v_s, _, _, z_s, xbc_s, dt_s = _in_proj(xs, g_mix, w_in_b, db)
    n_pages = page_table.shape[1]
    o_attn_s = _attn_paged(page_table, lam4, g_subln, q_s, k_s, v_s, cache_k, cache_v,
                           _pick(8, n_pages))
    cbuf_t = jnp.swapaxes(state_ssm_conv[0], 0, 1)
    o_ssm_s, st_s = _ssd_step(xbc_s, cbuf_t, z_s, dt_s,
                              state_ssm[0].reshape(db, SSM_WIDTH, SSM_STATE), *ssd_params, _pick(8, db))
    y_s, up_s = _ffn_sample(xs, o_attn_s, o_ssm_s, ffn_weights,
                            state_ffn_conv[0, :, 0], state_ffn_conv[0, :, 1])

    kv_shape_s = (1, db, 1, N_ATTN_HEADS, DA_V_DIM)
    outs_s = (y_s.reshape(db, 1, D_MODEL), k_s.reshape(kv_shape_s), v_s.reshape(kv_shape_s),
              st_s.reshape(1, db, N_SSM_HEADS, SSM_HEAD_DIM, SSM_STATE),
              jnp.concatenate([state_ssm_conv[:, :, 1:], xbc_s[None, :, None, :]], axis=2),
              jnp.concatenate([state_ffn_conv[:, :, 1:], up_s[None, :, None, :]], axis=2))

    return (outs_p[0], outs_s[0]) + outs_p[1:] + outs_s[1:]
```

```python
import functools
import math

import jax
import jax.numpy as jnp
from jax import lax
from jax.experimental import pallas as pl
from jax.experimental.pallas import tpu as pltpu

F32 = jnp.float32
BF16 = jnp.bfloat16

EPS = 1e-5
D_MODEL = 1024
ATTN_WIDTH = 512
DA_HEAD_DIM = 64
DA_V_DIM = 128
N_ATTN_HEADS = 4
SSM_WIDTH = 512
SSM_HEAD_DIM = 64
N_SSM_HEADS = 8
SSM_GROUPS = 2
SSM_HEADS_PER_GROUP = 4
SSM_STATE = 128
SSM_CONV = 4
SSM_CHUNK = 128
CONV_CH = 1024
D_FF = 2816
FFN_CONV = 3
PAGE_SIZE = 128
LANES = 128
DT_PAD = LANES
IN_COLS_PAD = 3 * ATTN_WIDTH + SSM_WIDTH + CONV_CH + DT_PAD
QK_SCALE = DA_HEAD_DIM ** -0.5 * math.log2(math.e)
LAM0 = 0.8 - 0.6 * math.exp(-0.3 * 0)
VMEM_LIMIT = 56 * 1024 * 1024

NT_DIMS = (((1,), (1,)), ((), ()))
TN_DIMS = (((0,), (0,)), ((), ()))


def _cparams(sem):
    return pltpu.CompilerParams(dimension_semantics=sem, vmem_limit_bytes=VMEM_LIMIT)


def _const_spec(shape):
    nd = len(shape)
    return pl.BlockSpec(shape, lambda *_: (0,) * nd, pipeline_mode=pl.Buffered(1))


def _sigmoid(x):
    return 1.0 / (1.0 + jnp.exp(-x))


def _softplus(x):
    return jnp.maximum(x, 0.0) + jnp.log1p(jnp.exp(-jnp.abs(x)))


def _lam_from_params(lam_ref):
    lp = lam_ref[...]
    t1 = jnp.sum(lp[0:1] * lp[1:2], axis=-1, keepdims=True)
    t2 = jnp.sum(lp[2:3] * lp[3:4], axis=-1, keepdims=True)
    return jnp.exp(t1) - jnp.exp(t2) + LAM0


def _inproj_body(x_ref, g_ref, w_ref, q_ref, k_ref, v_ref, kb_ref, vb_ref, z_ref, xbc_ref, dt_ref):
    x = x_ref[...]
    r = lax.rsqrt(jnp.mean(x * x, axis=-1, keepdims=True) + EPS)
    h = (x * r * g_ref[...]).astype(BF16)

    def proj(lo, hi):
        return jnp.dot(h, w_ref[:, lo:hi], preferred_element_type=F32)

    a = ATTN_WIDTH
    q_ref[...] = (proj(0, a) * QK_SCALE).astype(BF16)
    k = proj(a, 2 * a)
    kb_ref[...] = k.astype(BF16)
    v = proj(2 * a, 3 * a)
    vb_ref[...] = v.astype(BF16)
    tm = x.shape[0]
    for head in range(N_ATTN_HEADS):
        cols = slice(head * DA_V_DIM, (head + 1) * DA_V_DIM)
        k_ref[pl.ds(head, tm, stride=N_ATTN_HEADS), :] = k[:, cols]
        v_ref[pl.ds(head, tm, stride=N_ATTN_HEADS), :] = v[:, cols]
    z_ref[...] = proj(3 * a, 3 * a + SSM_WIDTH).astype(BF16)
    c0 = 3 * a + SSM_WIDTH
    xbc_ref[...] = proj(c0, c0 + CONV_CH)
    dt_ref[...] = proj(c0 + CONV_CH, c0 + CONV_CH + DT_PAD)


def _in_proj(x2d, g_mix, w_in_b, tm):
    n = x2d.shape[0]
    a = ATTN_WIDTH
    row = lambda w: pl.BlockSpec((tm, w), lambda i: (i, 0))
    out_shape = (
        jax.ShapeDtypeStruct((n, a), BF16),
        jax.ShapeDtypeStruct((n * N_ATTN_HEADS, DA_V_DIM), F32),
        jax.ShapeDtypeStruct((n * N_ATTN_HEADS, DA_V_DIM), F32),
        jax.ShapeDtypeStruct((n, a), BF16),
        jax.ShapeDtypeStruct((n, a), BF16),
        jax.ShapeDtypeStruct((n, SSM_WIDTH), BF16),
        jax.ShapeDtypeStruct((n, CONV_CH), F32),
        jax.ShapeDtypeStruct((n, DT_PAD), F32),
    )
    return pl.pallas_call(
        _inproj_body,
        grid=(n // tm,),
        in_specs=[row(D_MODEL), _const_spec((1, D_MODEL)), _const_spec((D_MODEL, IN_COLS_PAD))],
        out_specs=(row(a), pl.BlockSpec((tm * N_ATTN_HEADS, DA_V_DIM), lambda i: (i, 0)),
                   pl.BlockSpec((tm * N_ATTN_HEADS, DA_V_DIM), lambda i: (i, 0)),
                   row(a), row(a), row(SSM_WIDTH), row(CONV_CH), row(DT_PAD)),
        out_shape=out_shape,
        compiler_params=_cparams(("parallel",)),
        name="in_proj",
    )(x2d, g_mix, w_in_b)


def _attn_body(lam_ref, gsub_ref, q_ref, k_ref, v_ref, o_ref, *, t):
    i = pl.program_id(2)
    q = q_ref[...]
    lane = lax.broadcasted_iota(jnp.int32, q.shape, 1)
    zero = jnp.zeros_like(q)
    q1 = jnp.where(lane < DA_HEAD_DIM, q, zero)
    q2 = jnp.where(lane >= DA_HEAD_DIM, q, zero)

    def update(s, vt, m, l, acc):
        m_new = jnp.maximum(m, jnp.max(s, axis=-1, keepdims=True))
        alpha = jnp.exp2(m - m_new)
        p = jnp.exp2(s - m_new)
        l = alpha * l + jnp.sum(p, axis=-1, keepdims=True)
        acc = alpha * acc + jnp.dot(p.astype(BF16), vt, preferred_element_type=F32)
        return m_new, l, acc

    def scores(j, masked):
        start = pl.multiple_of(j * t, t)
        kt = k_ref[pl.ds(start, t), :]
        s1 = lax.dot_general(q1, kt, NT_DIMS, preferred_element_type=F32)
        s2 = lax.dot_general(q2, kt, NT_DIMS, preferred_element_type=F32)
        if masked:
            rr = lax.broadcasted_iota(jnp.int32, (t, t), 0)
            cc = lax.broadcasted_iota(jnp.int32, (t, t), 1)
            keep = cc <= rr
            s1 = jnp.where(keep, s1, -jnp.inf)
            s2 = jnp.where(keep, s2, -jnp.inf)
        return s1, s2, v_ref[pl.ds(start, t), :]

    def tiles(js, carry, masked):
        m1, l1, a1, m2, l2, a2 = carry
        for j in js:
            s1, s2, vt = scores(j, masked)
            m1, l1, a1 = update(s1, vt, m1, l1, a1)
            m2, l2, a2 = update(s2, vt, m2, l2, a2)
        return m1, l1, a1, m2, l2, a2

    ninf = jnp.full((t, 1), -jnp.inf, F32)
    zl = jnp.zeros((t, 1), F32)
    za = jnp.zeros((t, DA_V_DIM), F32)
    carry = (ninf, zl, za, ninf, zl, za)
    carry = lax.fori_loop(0, i // 2, lambda jj, c: tiles((2 * jj, 2 * jj + 1), c, False), carry)
    carry = lax.cond(i % 2 == 1, lambda c: tiles((i - 1,), c, False), lambda c: c, carry)
    m1, l1, a1, m2, l2, a2 = tiles((i,), carry, True)

    lam = _lam_from_params(lam_ref)
    o = a1 / l1 - lam * (a2 / l2)
    r = lax.rsqrt(jnp.mean(o * o, axis=-1, keepdims=True) + EPS)
    o = (o * r * gsub_ref[...]) * (1.0 - LAM0)
    o_ref[...] = o.astype(o_ref.dtype)


def _attn_prompt(q, kb, vb, lam4, g_subln, bsz, seq, t):
    nq = seq // t
    hb = pl.BlockSpec((t, DA_V_DIM), lambda b, h, i: (b * nq + i, h))
    kv = pl.BlockSpec((seq, DA_V_DIM), lambda b, h, i: (b, h))
    return pl.pallas_call(
        functools.partial(_attn_body, t=t),
        grid=(bsz, N_ATTN_HEADS, nq),
        in_specs=[_const_spec((4, DA_HEAD_DIM)), _const_spec((1, DA_V_DIM)), hb, kv, kv],
        out_specs=hb,
        out_shape=jax.ShapeDtypeStruct((bsz * seq, ATTN_WIDTH), BF16),
        compiler_params=_cparams(("parallel", "parallel", "arbitrary")),
        name="attn_prompt",
    )(lam4, g_subln, q, kb, vb)


PAGED_ROWS = 16


def _paged_body(pt_ref, lam_ref, gsub_ref, q_ref, kn_ref, vn_ref, *refs, pg):
    del pt_ref
    k_refs = refs[:pg]
    v_refs = refs[pg:2 * pg]
    o_ref = refs[2 * pg]
    m_scr, l_scr, acc_scr = refs[2 * pg + 1:]
    g = pl.program_id(1)
    rows = PAGED_ROWS
    nh = N_ATTN_HEADS

    row = lax.broadcasted_iota(jnp.int32, (rows, DA_V_DIM), 0)
    lane = lax.broadcasted_iota(jnp.int32, (rows, DA_V_DIM), 1)
    comp = lane // DA_HEAD_DIM == row
    qf = q_ref[0].astype(F32)

    def qmat(h):
        qh = qf[:, h * DA_V_DIM:(h + 1) * DA_V_DIM]
        return jnp.where(comp, jnp.broadcast_to(qh, (rows, DA_V_DIM)), 0.0)

    @pl.when(g == 0)
    def _():
        m_scr[...] = jnp.full(m_scr.shape, -jnp.inf, F32)
        l_scr[...] = jnp.zeros(l_scr.shape, F32)
        acc_scr[...] = jnp.zeros(acc_scr.shape, F32)

    m_all, l_all, acc_all = m_scr[...], l_scr[...], acc_scr[...]
    m_out, l_out, acc_out = [], [], []
    scores = []
    for h in range(nh):
        qm_b = qmat(h).astype(BF16)
        s_list = []
        for kr in k_refs:
            kh = kr[pl.ds(h, PAGE_SIZE, stride=nh), :].astype(BF16)
            s_list.append(lax.dot_general(qm_b, kh, NT_DIMS, preferred_element_type=F32))
        scores.append(s_list)
    probs = []
    for h in range(nh):
        hs = slice(h * rows, (h + 1) * rows)
        s_list = scores[h]
        s_max = s_list[0]
        for s in s_list[1:]:
            s_max = jnp.maximum(s_max, s)
        m_old = m_all[hs, :]
        m_new = jnp.maximum(m_old, jnp.max(s_max, axis=-1, keepdims=True))
        alpha = jnp.exp2(m_old - m_new)
        p_list = [jnp.exp2(s - m_new) for s in s_list]
        l_add = p_list[0]
        for p in p_list[1:]:
            l_add = l_add + p
        m_out.append(m_new)
        l_out.append(alpha * l_all[hs, :] + jnp.sum(l_add, axis=-1, keepdims=True))
        probs.append((alpha, [p.astype(BF16) for p in p_list]))
    for h in range(nh):
        hs = slice(h * rows, (h + 1) * rows)
        alpha, p_list = probs[h]
        pv = None
        for p, vr in zip(p_list, v_refs):
            vh = vr[pl.ds(h, PAGE_SIZE, stride=nh), :].astype(BF16)
            d = jnp.dot(p, vh, preferred_element_type=F32)
            pv = d if pv is None else pv + d
        acc_out.append(alpha * acc_all[hs, :] + pv)
    m_scr[...] = jnp.concatenate(m_out, axis=0)
    l_scr[...] = jnp.concatenate(l_out, axis=0)
    acc_scr[...] = jnp.concatenate(acc_out, axis=0)

    @pl.when(g == pl.num_programs(1) - 1)
    def _():
        lam = _lam_from_params(lam_ref)
        gs = gsub_ref[...]
        for h in range(nh):
            hs = slice(h * rows, (h + 1) * rows)
            cols = slice(h * DA_V_DIM, (h + 1) * DA_V_DIM)
            s_self = jnp.sum(qmat(h) * kn_ref[0][:, cols], axis=-1, keepdims=True)
            m_old = m_scr[hs, :]
            m_fin = jnp.maximum(m_old, s_self)
            a_fin = jnp.exp2(m_old - m_fin)
            p_self = jnp.exp2(s_self - m_fin)
            l_fin = a_fin * l_scr[hs, :] + p_self
            acc = (a_fin * acc_scr[hs, :] + p_self * vn_ref[0][:, cols]) / l_fin
            oh = acc[0:1] - lam * acc[1:2]
            r = lax.rsqrt(jnp.mean(oh * oh, axis=-1, keepdims=True) + EPS)
            o_ref[0, :, cols] = ((oh * r * gs) * (1.0 - LAM0)).astype(o_ref.dtype)


def _attn_paged(page_table, lam4, g_subln, q, k_new, v_new, cache_k, cache_v, pg):
    db, n_pages = page_table.shape
    n_groups = n_pages // pg
    page_rows = PAGE_SIZE * N_ATTN_HEADS
    ck = cache_k.reshape(-1, DA_V_DIM)
    cv = cache_v.reshape(-1, DA_V_DIM)
    tok = pl.BlockSpec((1, 1, ATTN_WIDTH), lambda b, g, pt: (b, 0, 0))

    def page_spec(i):
        return pl.BlockSpec((page_rows, DA_V_DIM), lambda b, g, pt: (pt[b, g * pg + i], 0))

    grid_spec = pltpu.PrefetchScalarGridSpec(
        num_scalar_prefetch=1,
        grid=(db, n_groups),
        in_specs=[pl.BlockSpec((4, DA_HEAD_DIM), lambda b, g, pt: (0, 0)),
                  pl.BlockSpec((1, DA_V_DIM), lambda b, g, pt: (0, 0)),
                  tok, tok, tok]
                 + [page_spec(i) for i in range(pg)] + [page_spec(i) for i in range(pg)],
        out_specs=tok,
        scratch_shapes=[pltpu.VMEM((N_ATTN_HEADS * PAGED_ROWS, 1), F32),
                        pltpu.VMEM((N_ATTN_HEADS * PAGED_ROWS, 1), F32),
                        pltpu.VMEM((N_ATTN_HEADS * PAGED_ROWS, DA_V_DIM), F32)],
    )
    out = pl.pallas_call(
        functools.partial(_paged_body, pg=pg),
        grid_spec=grid_spec,
        out_shape=jax.ShapeDtypeStruct((db, 1, ATTN_WIDTH), BF16),
        compiler_params=_cparams(("parallel", "arbitrary")),
        name="attn_paged",
    )(page_table, lam4, g_subln, q.reshape(db, 1, ATTN_WIDTH), k_new.reshape(db, 1, ATTN_WIDTH),
      v_new.reshape(db, 1, ATTN_WIDTH), *([ck] * pg), *([cv] * pg))
    return out.reshape(db, ATTN_WIDTH)


def _gate_norm(y, xs, z, dskip, gssm):
    y = y + dskip * xs
    zf = z.astype(F32)
    yg = y * (zf * _sigmoid(zf))
    gw = SSM_WIDTH // SSM_GROUPS
    outs = []
    for g in range(SSM_GROUPS):
        part = yg[:, g * gw:(g + 1) * gw]
        r = lax.rsqrt(jnp.mean(part * part, axis=-1, keepdims=True) + EPS)
        outs.append(part * r * gssm[:, g * gw:(g + 1) * gw])
    return jnp.concatenate(outs, axis=-1)


def _ssd_body(xbc_ref, z_ref, dt_ref, cw_ref, cb_ref, dtb_ref, alog_ref, dskip_ref, gssm_ref,
              o_ref, st_ref, ext_ref, *, lb):
    j = pl.program_id(1)
    L = SSM_CHUNK
    hpg = SSM_HEADS_PER_GROUP
    gw = hpg * SSM_HEAD_DIM
    pad = 8

    @pl.when(j == 0)
    def _():
        ext_ref[0:pad, :] = jnp.zeros((pad, CONV_CH), F32)
        st_ref[...] = jnp.zeros(st_ref.shape, F32)

    ext_ref[pad:pad + lb, :] = xbc_ref[...]

    cw = cw_ref[...]
    cbias = cb_ref[...]
    a_neg = -jnp.exp(alog_ref[...])
    dtb = dtb_ref[...]
    dskip = dskip_ref[...]
    gssm = gssm_ref[...]
    rr = lax.broadcasted_iota(jnp.int32, (L, L), 0)
    cc = lax.broadcasted_iota(jnp.int32, (L, L), 1)
    causal = cc <= rr
    tri = causal.astype(F32)
    lane_g = lax.broadcasted_iota(jnp.int32, (L, gw), 1) // SSM_HEAD_DIM

    for c in range(lb // L):
        r0 = c * L
        xc = cbias + cw[3:4] * ext_ref[pad + r0:pad + r0 + L, :]
        for k in range(1, SSM_CONV):
            xc = xc + cw[3 - k:4 - k] * ext_ref[pad + r0 - k:pad + r0 - k + L, :]
        xc = xc * _sigmoid(xc)
        xs = xc[:, :SSM_WIDTH]
        dt = _softplus(dt_ref[r0:r0 + L, :] + dtb)
        a = dt * a_neg
        cs = jnp.dot(tri, a, preferred_element_type=F32, precision=lax.Precision.HIGHEST)
        cs_t = cs.T
        dt_t = dt.T
        ecs = jnp.exp(cs)
        wb = jnp.exp(cs[L - 1:L, :] - cs) * dt
        e_last = jnp.exp(cs_t[:, L - 1:L])

        y_parts = []
        for g in range(SSM_GROUPS):
            bg = xc[:, SSM_WIDTH + g * SSM_STATE:SSM_WIDTH + (g + 1) * SSM_STATE].astype(BF16)
            cg = xc[:, SSM_WIDTH + (SSM_GROUPS + g) * SSM_STATE:
                    SSM_WIDTH + (SSM_GROUPS + g + 1) * SSM_STATE].astype(BF16)
            xg = xs[:, g * gw:(g + 1) * gw]
            xg_b = xg.astype(BF16)
            st_g = st_ref[g * gw:(g + 1) * gw, :]
            cbm = lax.dot_general(cg, bg, NT_DIMS, preferred_element_type=F32)
            ys = lax.dot_general(cg, st_g.astype(BF16), NT_DIMS, preferred_element_type=F32)
            yg = jnp.zeros((L, gw), F32)
            ecs_g = jnp.zeros((L, gw), F32)
            wb_g = jnp.zeros((L, gw), F32)
            scale_rows = []
            for r in range(hpg):
                h = g * hpg + r
                seg = cs[:, h:h + 1] - cs_t[h:h + 1, :]
                decay = jnp.exp(jnp.where(causal, seg, -jnp.inf))
                w = (cbm * decay * dt_t[h:h + 1, :]).astype(BF16)
                yh = jnp.dot(w, xg_b, preferred_element_type=F32)
                sel = lane_g == r
                yg = jnp.where(sel, yh, yg)
                ecs_g = jnp.where(sel, ecs[:, h:h + 1], ecs_g)
                wb_g = jnp.where(sel, wb[:, h:h + 1], wb_g)
                scale_rows.append(jnp.broadcast_to(e_last[h:h + 1, :], (SSM_HEAD_DIM, SSM_STATE)))
            y_parts.append(yg + ecs_g * ys)
            xw = (xg * wb_g).astype(BF16)
            upd = lax.dot_general(xw, bg, TN_DIMS, preferred_element_type=F32)
            st_ref[g * gw:(g + 1) * gw, :] = jnp.concatenate(scale_rows, axis=0) * st_g + upd
        y = jnp.concatenate(y_parts, axis=-1)
        o_ref[r0:r0 + L, :] = _gate_norm(y, xs, z_ref[r0:r0 + L, :], dskip, gssm).astype(o_ref.dtype)

    ext_ref[0:pad, :] = ext_ref[lb:lb + pad, :]


def _ssd_prompt(xbc, z, dt, cw, cb, dtb, alog, dskip, gssm, bsz, seq, lb):
    nb = seq // lb
    row = lambda w: pl.BlockSpec((lb, w), lambda b, j: (b * nb + j, 0))
    return pl.pallas_call(
        functools.partial(_ssd_body, lb=lb),
        grid=(bsz, nb),
        in_specs=[row(CONV_CH), row(SSM_WIDTH), row(DT_PAD),
                  _const_spec((SSM_CONV, CONV_CH)), _const_spec((1, CONV_CH)),
                  _const_spec((1, DT_PAD)), _const_spec((1, DT_PAD)),
                  _const_spec((1, SSM_WIDTH)), _const_spec((1, SSM_WIDTH))],
        out_specs=(row(SSM_WIDTH),
                   pl.BlockSpec((None, SSM_WIDTH, SSM_STATE), lambda b, j: (b, 0, 0))),
        out_shape=(jax.ShapeDtypeStruct((bsz * seq, SSM_WIDTH), BF16),
                   jax.ShapeDtypeStruct((bsz, SSM_WIDTH, SSM_STATE), F32)),
        scratch_shapes=[pltpu.VMEM((lb + 8, CONV_CH), F32)],
        compiler_params=_cparams(("parallel", "arbitrary")),
        name="ssd_prompt",
    )(xbc, z, dt, cw, cb, dtb, alog, dskip, gssm)


def _ssd_step_body(xbc_ref, cbuf_ref, z_ref, dt_ref, st_ref, cw_ref, cb_ref, dtb_ref, alog_ref,
                   dskip_ref, gssm_ref, o_ref, sto_ref, y_scr, *, bb):
    hpg = SSM_HEADS_PER_GROUP
    gw = hpg * SSM_HEAD_DIM
    cw = cw_ref[...]
    xc = cb_ref[...] + cw[3:4] * xbc_ref[...]
    for k in range(SSM_CONV - 1):
        xc = xc + cw[k:k + 1] * cbuf_ref[k]
    xc = xc * _sigmoid(xc)
    xs = xc[:, :SSM_WIDTH]
    dt = _softplus(dt_ref[...] + dtb_ref[...])
    da = jnp.exp(dt * (-jnp.exp(alog_ref[...])))
    zpad = jnp.zeros((LANES - bb, LANES), F32)
    xs_t = [jnp.concatenate([xs[:, k * LANES:(k + 1) * LANES], zpad], axis=0).T
            for k in range(SSM_WIDTH // LANES)]
    lane8 = lax.broadcasted_iota(jnp.int32, (8, SSM_WIDTH), 1)
    row8 = lax.broadcasted_iota(jnp.int32, (8, SSM_STATE), 0)

    for b in range(bb):
        new_rows = []
        for h in range(N_SSM_HEADS):
            g = h // hpg
            xt = xs_t[h // 2]
            xcol = xt[(h % 2) * SSM_HEAD_DIM:(h % 2 + 1) * SSM_HEAD_DIM, b:b + 1]
            brow = xc[b:b + 1, SSM_WIDTH + g * SSM_STATE:SSM_WIDTH + (g + 1) * SSM_STATE]
            old = st_ref[b, h * SSM_HEAD_DIM:(h + 1) * SSM_HEAD_DIM, :]
            new = da[b:b + 1, h:h + 1] * old + (dt[b:b + 1, h:h + 1] * xcol) * brow
            sto_ref[b, h * SSM_HEAD_DIM:(h + 1) * SSM_HEAD_DIM, :] = new
            new_rows.append(new.astype(BF16))
        new_all = jnp.concatenate(new_rows, axis=0)
        c0 = SSM_WIDTH + SSM_GROUPS * SSM_STATE
        cmat = jnp.where(row8 == 0, xc[b:b + 1, c0:c0 + SSM_STATE],
                         jnp.where(row8 == 1, xc[b:b + 1, c0 + SSM_STATE:c0 + 2 * SSM_STATE], 0.0))
        yy = lax.dot_general(cmat.astype(BF16), new_all, NT_DIMS, preferred_element_type=F32)
        y_scr[b:b + 1, :] = jnp.where(lane8[0:1] < gw, yy[0:1], yy[1:2])
    o_ref[...] = _gate_norm(y_scr[...], xs, z_ref[...], dskip_ref[...], gssm_ref[...]).astype(o_ref.dtype)


def _ssd_step(xbc, cbuf_t, z, dt, state, cw, cb, dtb, alog, dskip, gssm, bb):
    db = xbc.shape[0]
    row = lambda w: pl.BlockSpec((bb, w), lambda i: (i, 0))
    st = pl.BlockSpec((bb, SSM_WIDTH, SSM_STATE), lambda i: (i, 0, 0))
    return pl.pallas_call(
        functools.partial(_ssd_step_body, bb=bb),
        grid=(db // bb,),
        in_specs=[row(CONV_CH), pl.BlockSpec((SSM_CONV - 1, bb, CONV_CH), lambda i: (0, i, 0)),
                  row(SSM_WIDTH), row(DT_PAD), st,
                  _const_spec((SSM_CONV, CONV_CH)), _const_spec((1, CONV_CH)),
                  _const_spec((1, DT_PAD)), _const_spec((1, DT_PAD)),
                  _const_spec((1, SSM_WIDTH)), _const_spec((1, SSM_WIDTH))],
        out_specs=(row(SSM_WIDTH), st),
        out_shape=(jax.ShapeDtypeStruct((db, SSM_WIDTH), BF16),
                   jax.ShapeDtypeStruct((db, SSM_WIDTH, SSM_STATE), F32)),
        scratch_shapes=[pltpu.VMEM((bb, SSM_WIDTH), F32)],
        compiler_params=_cparams(("parallel",)),
        name="ssd_step",
    )(xbc, cbuf_t, z, dt, state, cw, cb, dtb, alog, dskip, gssm)


FFN_CHUNK = 256


def _ffn_body(*refs, tm, sample):
    if sample:
        (x_ref, oa_ref, os_ref, wo_ref, gf_ref, wu_ref, cw_ref, cb_ref, wd_ref, gfin_ref,
         p2_ref, p1_ref, y_ref, up_ref) = refs
    else:
        (x_ref, oa_ref, os_ref, wo_ref, gf_ref, wu_ref, cw_ref, cb_ref, wd_ref, gfin_ref,
         y_ref, carry_ref, extg_ref, extv_ref) = refs
        pad = 8

        @pl.when(pl.program_id(1) == 0)
        def _():
            carry_ref[...] = jnp.zeros(carry_ref.shape, F32)

    x1 = (x_ref[...]
          + jnp.dot(oa_ref[...], wo_ref[0:ATTN_WIDTH, :], preferred_element_type=F32)
          + jnp.dot(os_ref[...], wo_ref[ATTN_WIDTH:, :], preferred_element_type=F32))
    r = lax.rsqrt(jnp.mean(x1 * x1, axis=-1, keepdims=True) + EPS)
    h2 = (x1 * r * gf_ref[...]).astype(BF16)
    cw = cw_ref[...]
    cbias = cb_ref[...]

    acc = jnp.zeros_like(x1)
    for c in range(D_FF // FFN_CHUNK):
        parts = []
        for lo in (c * FFN_CHUNK, D_FF + c * FFN_CHUNK):
            hi = lo + FFN_CHUNK
            u = jnp.dot(h2, wu_ref[:, lo:hi], preferred_element_type=F32)
            if sample:
                up_ref[:, lo:hi] = u
                u1 = p1_ref[:, lo:hi]
                u2 = p2_ref[:, lo:hi]
            else:
                ext = extg_ref if lo < D_FF else extv_ref
                ext[0:pad, :] = carry_ref[:, lo:hi]
                ext[pad:pad + tm, :] = u
                u1 = ext[pad - 1:pad - 1 + tm, :]
                u2 = ext[pad - 2:pad - 2 + tm, :]
                carry_ref[:, lo:hi] = ext[tm:tm + pad, :]
            parts.append(cbias[:, lo:hi] + cw[0:1, lo:hi] * u2 + cw[1:2, lo:hi] * u1 + cw[2:3, lo:hi] * u)
        gate, val = parts
        act = ((gate * _sigmoid(gate)) * val).astype(BF16)
        acc = acc + jnp.dot(act, wd_ref[c * FFN_CHUNK:(c + 1) * FFN_CHUNK, :], preferred_element_type=F32)
    x2 = x1 + acc
    r2 = lax.rsqrt(jnp.mean(x2 * x2, axis=-1, keepdims=True) + EPS)
    y_ref[...] = x2 * r2 * gfin_ref[...]


def _ffn_weights_specs():
    return [_const_spec((D_MODEL, D_MODEL)), _const_spec((1, D_MODEL)),
            _const_spec((D_MODEL, 2 * D_FF)), _const_spec((FFN_CONV, 2 * D_FF)),
            _const_spec((1, 2 * D_FF)), _const_spec((D_FF, D_MODEL)), _const_spec((1, D_MODEL))]


def _ffn_prompt(x2d, oa, os_, weights, bsz, seq, tm):
    nt = seq // tm
    row = lambda w: pl.BlockSpec((tm, w), lambda b, i: (b * nt + i, 0))
    return pl.pallas_call(
        functools.partial(_ffn_body, tm=tm, sample=False),
        grid=(bsz, nt),
        in_specs=[row(D_MODEL), row(ATTN_WIDTH), row(SSM_WIDTH)] + _ffn_weights_specs(),
        out_specs=(row(D_MODEL), pl.BlockSpec((None, 8, 2 * D_FF), lambda b, i: (b, 0, 0))),
        out_shape=(jax.ShapeDtypeStruct((bsz * seq, D_MODEL), F32),
                   jax.ShapeDtypeStruct((bsz, 8, 2 * D_FF), F32)),
        scratch_shapes=[pltpu.VMEM((tm + 8, FFN_CHUNK), F32), pltpu.VMEM((tm + 8, FFN_CHUNK), F32)],
        compiler_params=_cparams(("parallel", "arbitrary")),
        name="ffn_prompt",
    )(x2d, oa, os_, *weights)


def _ffn_sample(x2d, oa, os_, weights, prev2, prev1):
    db = x2d.shape[0]
    full = lambda w: pl.BlockSpec((db, w), lambda i: (0, 0))
    return pl.pallas_call(
        functools.partial(_ffn_body, tm=db, sample=True),
        grid=(1,),
        in_specs=[full(D_MODEL), full(ATTN_WIDTH), full(SSM_WIDTH)] + _ffn_weights_specs()
                 + [full(2 * D_FF), full(2 * D_FF)],
        out_specs=(full(D_MODEL), full(2 * D_FF)),
        out_shape=(jax.ShapeDtypeStruct((db, D_MODEL), F32),
                   jax.ShapeDtypeStruct((db, 2 * D_FF), F32)),
        compiler_params=_cparams(("arbitrary",)),
        name="ffn_sample",
    )(x2d, oa, os_, *weights, prev2, prev1)


def _pick(pref, n):
    t = min(pref, n)
    assert n % t == 0, (pref, n)
    return t


def kernel(x_prompt, x_sample, cache_k, cache_v, page_table, state_ssm, state_ssm_conv,
           state_ffn_conv, g_mix, w_in, lam_q1, lam_k1, lam_q2, lam_k2, g_subln,
           ssm_conv_w, ssm_conv_b, dt_bias, A_log, D_skip, g_ssm, w_out, g_ffn, w_up,
           ffn_conv_w, ffn_conv_b, w_down, g_final):
    bsz, seq, _ = x_prompt.shape
    db = x_sample.shape[0]
    assert x_sample.shape[1] == 1 and w_in.shape[0] == 1

    in_cols = w_in.shape[-1]
    w_in_b = jnp.pad(w_in[0], ((0, 0), (0, IN_COLS_PAD - in_cols))).astype(BF16)
    lam4 = jnp.concatenate([lam_q1, lam_k1, lam_q2, lam_k2], axis=0)
    head_pad = lambda v: jnp.pad(v, ((0, 0), (0, DT_PAD - N_SSM_HEADS)))
    dtb, alog = head_pad(dt_bias), head_pad(A_log)
    dskip = jnp.repeat(D_skip, SSM_HEAD_DIM, axis=-1)
    ssd_params = (ssm_conv_w[0], ssm_conv_b, dtb, alog, dskip, g_ssm)
    ffn_weights = (w_out[0].astype(BF16), g_ffn, w_up[0].astype(BF16), ffn_conv_w[0], ffn_conv_b,
                   w_down[0].astype(BF16), g_final.reshape(1, D_MODEL))

    n = bsz * seq
    xp = x_prompt.reshape(n, D_MODEL)
    q, k, v, kb, vb, z, xbc, dt = _in_proj(xp, g_mix, w_in_b, _pick(512, n))
    o_attn = _attn_prompt(q, kb, vb, lam4, g_subln, bsz, seq, _pick(1024, seq))
    o_ssm, st_p = _ssd_prompt(xbc, z, dt, *ssd_params, bsz, seq, _pick(512, seq))
    y_p, ffn_carry = _ffn_prompt(xp, o_attn, o_ssm, ffn_weights, bsz, seq, _pick(512, seq))

    kv_shape = (1, bsz, seq, N_ATTN_HEADS, DA_V_DIM)
    outs_p = (y_p.reshape(bsz, seq, D_MODEL), k.reshape(kv_shape), v.reshape(kv_shape),
              st_p.reshape(1, bsz, N_SSM_HEADS, SSM_HEAD_DIM, SSM_STATE),
              xbc.reshape(bsz, seq, CONV_CH)[None, :, seq - (SSM_CONV - 1):],
              ffn_carry[None, :, 8 - (FFN_CONV - 1):])

    xs = x_sample.reshape(db, D_MODEL)
    q_s, k_s, v_s, _, _, z_s, xbc_s, dt_s = _in_proj(xs, g_mix, w_in_b, db)
    n_pages = page_table.shape[1]
    o_attn_s = _attn_paged(page_table, lam4, g_subln, q_s, k_s, v_s, cache_k, cache_v,
                           _pick(16, n_pages))
    cbuf_t = jnp.swapaxes(state_ssm_conv[0], 0, 1)
    o_ssm_s, st_s = _ssd_step(xbc_s, cbuf_t, z_s, dt_s,
                              state_ssm[0].reshape(db, SSM_WIDTH, SSM_STATE), *ssd_params, _pick(8, db))
    y_s, up_s = _ffn_sample(xs, o_attn_s, o_ssm_s, ffn_weights,
                            state_ffn_conv[0, :, 0], state_ffn_conv[0, :, 1])

    kv_shape_s = (1, db, 1, N_ATTN_HEADS, DA_V_DIM)
    outs_s = (y_s.reshape(db, 1, D_MODEL), k_s.reshape(kv_shape_s), v_s.reshape(kv_shape_s),
              st_s.reshape(1, db, N_SSM_HEADS, SSM_HEAD_DIM, SSM_STATE),
              jnp.concatenate([state_ssm_conv[:, :, 1:], xbc_s[None, :, None, :]], axis=2),
              jnp.concatenate([state_ffn_conv[:, :, 1:], up_s[None, :, None, :]], axis=2))

    return (outs_p[0], outs_s[0]) + outs_p[1:] + outs_s[1:]
```

```python
import functools
import math

import jax
import jax.numpy as jnp
from jax import lax
from jax.experimental import pallas as pl
from jax.experimental.pallas import tpu as pltpu

F32 = jnp.float32
BF16 = jnp.bfloat16

EPS = 1e-5
D_MODEL = 1024
ATTN_WIDTH = 512
DA_HEAD_DIM = 64
DA_V_DIM = 128
N_ATTN_HEADS = 4
SSM_WIDTH = 512
SSM_HEAD_DIM = 64
N_SSM_HEADS = 8
SSM_GROUPS = 2
SSM_HEADS_PER_GROUP = 4
SSM_STATE = 128
SSM_CONV = 4
SSM_CHUNK = 128
CONV_CH = 1024
D_FF = 2816
FFN_CONV = 3
PAGE_SIZE = 128
LANES = 128
DT_PAD = LANES
IN_COLS_PAD = 3 * ATTN_WIDTH + SSM_WIDTH + CONV_CH + DT_PAD
QK_SCALE = DA_HEAD_DIM ** -0.5 * math.log2(math.e)
LAM0 = 0.8 - 0.6 * math.exp(-0.3 * 0)
VMEM_LIMIT = 56 * 1024 * 1024

NT_DIMS = (((1,), (1,)), ((), ()))
TN_DIMS = (((0,), (0,)), ((), ()))


def _cparams(sem):
    return pltpu.CompilerParams(dimension_semantics=sem, vmem_limit_bytes=VMEM_LIMIT)


def _const_spec(shape):
    nd = len(shape)
    return pl.BlockSpec(shape, lambda *_: (0,) * nd, pipeline_mode=pl.Buffered(1))


def _sigmoid(x):
    return 1.0 / (1.0 + jnp.exp(-x))


def _softplus(x):
    return jnp.maximum(x, 0.0) + jnp.log1p(jnp.exp(-jnp.abs(x)))


def _lam_from_params(lam_ref):
    lp = lam_ref[...]
    t1 = jnp.sum(lp[0:1] * lp[1:2], axis=-1, keepdims=True)
    t2 = jnp.sum(lp[2:3] * lp[3:4], axis=-1, keepdims=True)
    return jnp.exp(t1) - jnp.exp(t2) + LAM0


def _inproj_body(x_ref, g_ref, w_ref, q_ref, k_ref, v_ref, kb_ref, vb_ref, z_ref, xbc_ref, dt_ref):
    x = x_ref[...]
    r = lax.rsqrt(jnp.mean(x * x, axis=-1, keepdims=True) + EPS)
    h = (x * r * g_ref[...]).astype(BF16)

    def proj(lo, hi):
        return jnp.dot(h, w_ref[:, lo:hi], preferred_element_type=F32)

    a = ATTN_WIDTH
    q_ref[...] = (proj(0, a) * QK_SCALE).astype(BF16)
    k = proj(a, 2 * a)
    kb_ref[...] = k.astype(BF16)
    v = proj(2 * a, 3 * a)
    vb_ref[...] = v.astype(BF16)
    tm = x.shape[0]
    for head in range(N_ATTN_HEADS):
        cols = slice(head * DA_V_DIM, (head + 1) * DA_V_DIM)
        k_ref[pl.ds(head, tm, stride=N_ATTN_HEADS), :] = k[:, cols]
        v_ref[pl.ds(head, tm, stride=N_ATTN_HEADS), :] = v[:, cols]
    z_ref[...] = proj(3 * a, 3 * a + SSM_WIDTH).astype(BF16)
    c0 = 3 * a + SSM_WIDTH
    xbc_ref[...] = proj(c0, c0 + CONV_CH)
    dt_ref[...] = proj(c0 + CONV_CH, c0 + CONV_CH + DT_PAD)


def _in_proj(x2d, g_mix, w_in_b, tm):
    n = x2d.shape[0]
    a = ATTN_WIDTH
    row = lambda w: pl.BlockSpec((tm, w), lambda i: (i, 0))
    out_shape = (
        jax.ShapeDtypeStruct((n, a), BF16),
        jax.ShapeDtypeStruct((n * N_ATTN_HEADS, DA_V_DIM), F32),
        jax.ShapeDtypeStruct((n * N_ATTN_HEADS, DA_V_DIM), F32),
        jax.ShapeDtypeStruct((n, a), BF16),
        jax.ShapeDtypeStruct((n, a), BF16),
        jax.ShapeDtypeStruct((n, SSM_WIDTH), BF16),
        jax.ShapeDtypeStruct((n, CONV_CH), F32),
        jax.ShapeDtypeStruct((n, DT_PAD), F32),
    )
    return pl.pallas_call(
        _inproj_body,
        grid=(n // tm,),
        in_specs=[row(D_MODEL), _const_spec((1, D_MODEL)), _const_spec((D_MODEL, IN_COLS_PAD))],
        out_specs=(row(a), pl.BlockSpec((tm * N_ATTN_HEADS, DA_V_DIM), lambda i: (i, 0)),
                   pl.BlockSpec((tm * N_ATTN_HEADS, DA_V_DIM), lambda i: (i, 0)),
                   row(a), row(a), row(SSM_WIDTH), row(CONV_CH), row(DT_PAD)),
        out_shape=out_shape,
        compiler_params=_cparams(("parallel",)),
        name="in_proj",
    )(x2d, g_mix, w_in_b)


def _attn_body(lam_ref, gsub_ref, q_ref, k_ref, v_ref, o_ref, *, t):
    i = pl.program_id(2)
    q = q_ref[...]
    lane = lax.broadcasted_iota(jnp.int32, q.shape, 1)
    zero = jnp.zeros_like(q)
    q1 = jnp.where(lane < DA_HEAD_DIM, q, zero)
    q2 = jnp.where(lane >= DA_HEAD_DIM, q, zero)

    def update(s, vt, m, l, acc):
        m_new = jnp.maximum(m, jnp.max(s, axis=-1, keepdims=True))
        alpha = jnp.exp2(m - m_new)
        p = jnp.exp2(s - m_new)
        l = alpha * l + jnp.sum(p, axis=-1, keepdims=True)
        acc = alpha * acc + jnp.dot(p.astype(BF16), vt, preferred_element_type=F32)
        return m_new, l, acc

    def scores(j, masked):
        start = pl.multiple_of(j * t, t)
        kt = k_ref[pl.ds(start, t), :]
        s1 = lax.dot_general(q1, kt, NT_DIMS, preferred_element_type=F32)
        s2 = lax.dot_general(q2, kt, NT_DIMS, preferred_element_type=F32)
        if masked:
            rr = lax.broadcasted_iota(jnp.int32, (t, t), 0)
            cc = lax.broadcasted_iota(jnp.int32, (t, t), 1)
            keep = cc <= rr
            s1 = jnp.where(keep, s1, -jnp.inf)
            s2 = jnp.where(keep, s2, -jnp.inf)
        return s1, s2, v_ref[pl.ds(start, t), :]

    def tiles(js, carry, masked):
        m1, l1, a1, m2, l2, a2 = carry
        for j in js:
            s1, s2, vt = scores(j, masked)
            m1, l1, a1 = update(s1, vt, m1, l1, a1)
            m2, l2, a2 = update(s2, vt, m2, l2, a2)
        return m1, l1, a1, m2, l2, a2

    ninf = jnp.full((t, 1), -jnp.inf, F32)
    zl = jnp.zeros((t, 1), F32)
    za = jnp.zeros((t, DA_V_DIM), F32)
    carry = (ninf, zl, za, ninf, zl, za)
    carry = lax.fori_loop(0, i // 2, lambda jj, c: tiles((2 * jj, 2 * jj + 1), c, False), carry)
    carry = lax.cond(i % 2 == 1, lambda c: tiles((i - 1,), c, False), lambda c: c, carry)
    m1, l1, a1, m2, l2, a2 = tiles((i,), carry, True)

    lam = _lam_from_params(lam_ref)
    o = a1 / l1 - lam * (a2 / l2)
    r = lax.rsqrt(jnp.mean(o * o, axis=-1, keepdims=True) + EPS)
    o = (o * r * gsub_ref[...]) * (1.0 - LAM0)
    o_ref[...] = o.astype(o_ref.dtype)


def _attn_prompt(q, kb, vb, lam4, g_subln, bsz, seq, t):
    nq = seq // t
    hb = pl.BlockSpec((t, DA_V_DIM), lambda b, h, i: (b * nq + i, h))
    kv = pl.BlockSpec((seq, DA_V_DIM), lambda b, h, i: (b, h))
    return pl.pallas_call(
        functools.partial(_attn_body, t=t),
        grid=(bsz, N_ATTN_HEADS, nq),
        in_specs=[_const_spec((4, DA_HEAD_DIM)), _const_spec((1, DA_V_DIM)), hb, kv, kv],
        out_specs=hb,
        out_shape=jax.ShapeDtypeStruct((bsz * seq, ATTN_WIDTH), BF16),
        compiler_params=_cparams(("parallel", "parallel", "arbitrary")),
        name="attn_prompt",
    )(lam4, g_subln, q, kb, vb)


PAGED_ROWS = 16
PAGED_SLOTS = 3


def _paged_body(pt_ref, lam_ref, gsub_ref, q_ref, kn_ref, vn_ref, ck_hbm, cv_hbm, o_ref,
                kbuf, vbuf, sems, m_scr, l_scr, acc_scr, *, pg):
    b = pl.program_id(0)
    g = pl.program_id(1)
    n_groups = pl.num_programs(1)
    step = b * n_groups + g
    n_steps = pl.num_programs(0) * n_groups
    rows = PAGED_ROWS
    nh = N_ATTN_HEADS
    page_rows = PAGE_SIZE * nh

    def group_copies(st, slot):
        bb = st // n_groups
        gg = st % n_groups
        out = []
        for i in range(pg):
            row0 = pl.multiple_of(pt_ref[bb, gg * pg + i] * page_rows, page_rows)
            out.append(pltpu.make_async_copy(ck_hbm.at[pl.ds(row0, page_rows), :], kbuf.at[slot, i], sems.at[0, slot]))
            out.append(pltpu.make_async_copy(cv_hbm.at[pl.ds(row0, page_rows), :], vbuf.at[slot, i], sems.at[1, slot]))
        return out

    @pl.when(step == 0)
    def _():
        for st in range(PAGED_SLOTS - 1):
            @pl.when(st < n_steps)
            def _():
                for cp in group_copies(st, st):
                    cp.start()

    ahead = step + (PAGED_SLOTS - 1)

    @pl.when(ahead < n_steps)
    def _():
        for cp in group_copies(ahead, ahead % PAGED_SLOTS):
            cp.start()

    slot = step % PAGED_SLOTS
    for cp in group_copies(step, slot):
        cp.wait()

    row = lax.broadcasted_iota(jnp.int32, (rows, DA_V_DIM), 0)
    lane = lax.broadcasted_iota(jnp.int32, (rows, DA_V_DIM), 1)
    comp = lane // DA_HEAD_DIM == row
    qf = q_ref[0].astype(F32)

    def qmat(h):
        qh = qf[:, h * DA_V_DIM:(h + 1) * DA_V_DIM]
        return jnp.where(comp, jnp.broadcast_to(qh, (rows, DA_V_DIM)), 0.0)

    @pl.when(g == 0)
    def _():
        m_scr[...] = jnp.full(m_scr.shape, -jnp.inf, F32)
        l_scr[...] = jnp.zeros(l_scr.shape, F32)
        acc_scr[...] = jnp.zeros(acc_scr.shape, F32)

    m_all, l_all, acc_all = m_scr[...], l_scr[...], acc_scr[...]
    m_out, l_out, acc_out = [], [], []
    scores = []
    for h in range(nh):
        qm_b = qmat(h).astype(BF16)
        s_list = []
        for i in range(pg):
            kh = kbuf[slot, i, pl.ds(h, PAGE_SIZE, stride=nh), :].astype(BF16)
            s_list.append(lax.dot_general(qm_b, kh, NT_DIMS, preferred_element_type=F32))
        scores.append(s_list)
    probs = []
    for h in range(nh):
        hs = slice(h * rows, (h + 1) * rows)
        s_list = scores[h]
        s_max = s_list[0]
        for s in s_list[1:]:
            s_max = jnp.maximum(s_max, s)
        m_old = m_all[hs, :]
        m_new = jnp.maximum(m_old, jnp.max(s_max, axis=-1, keepdims=True))
        alpha = jnp.exp2(m_old - m_new)
        p_list = [jnp.exp2(s - m_new) for s in s_list]
        l_add = p_list[0]
        for p in p_list[1:]:
            l_add = l_add + p
        m_out.append(m_new)
        l_out.append(alpha * l_all[hs, :] + jnp.sum(l_add, axis=-1, keepdims=True))
        probs.append((alpha, [p.astype(BF16) for p in p_list]))
    for h in range(nh):
        hs = slice(h * rows, (h + 1) * rows)
        alpha, p_list = probs[h]
        pv = None
        for i, p in enumerate(p_list):
            vh = vbuf[slot, i, pl.ds(h, PAGE_SIZE, stride=nh), :].astype(BF16)
            d = jnp.dot(p, vh, preferred_element_type=F32)
            pv = d if pv is None else pv + d
        acc_out.append(alpha * acc_all[hs, :] + pv)
    m_scr[...] = jnp.concatenate(m_out, axis=0)
    l_scr[...] = jnp.concatenate(l_out, axis=0)
    acc_scr[...] = jnp.concatenate(acc_out, axis=0)

    @pl.when(g == n_groups - 1)
    def _():
        lam = _lam_from_params(lam_ref)
        gs = gsub_ref[...]
        for h in range(nh):
            hs = slice(h * rows, (h + 1) * rows)
            cols = slice(h * DA_V_DIM, (h + 1) * DA_V_DIM)
            s_self = jnp.sum(qmat(h) * kn_ref[0][:, cols], axis=-1, keepdims=True)
            m_old = m_scr[hs, :]
            m_fin = jnp.maximum(m_old, s_self)
            a_fin = jnp.exp2(m_old - m_fin)
            p_self = jnp.exp2(s_self - m_fin)
            l_fin = a_fin * l_scr[hs, :] + p_self
            acc = (a_fin * acc_scr[hs, :] + p_self * vn_ref[0][:, cols]) / l_fin
            oh = acc[0:1] - lam * acc[1:2]
            r = lax.rsqrt(jnp.mean(oh * oh, axis=-1, keepdims=True) + EPS)
            o_ref[0, :, cols] = ((oh * r * gs) * (1.0 - LAM0)).astype(o_ref.dtype)


def _attn_paged(page_table, lam4, g_subln, q, k_new, v_new, cache_k, cache_v, pg):
    db, n_pages = page_table.shape
    n_groups = n_pages // pg
    page_rows = PAGE_SIZE * N_ATTN_HEADS
    ck = cache_k.reshape(-1, DA_V_DIM)
    cv = cache_v.reshape(-1, DA_V_DIM)
    tok = pl.BlockSpec((1, 1, ATTN_WIDTH), lambda b, g, pt: (b, 0, 0))
    hbm = pl.BlockSpec(memory_space=pl.ANY)
    stat_rows = N_ATTN_HEADS * PAGED_ROWS
    grid_spec = pltpu.PrefetchScalarGridSpec(
        num_scalar_prefetch=1,
        grid=(db, n_groups),
        in_specs=[pl.BlockSpec((4, DA_HEAD_DIM), lambda b, g, pt: (0, 0)),
                  pl.BlockSpec((1, DA_V_DIM), lambda b, g, pt: (0, 0)),
                  tok, tok, tok, hbm, hbm],
        out_specs=tok,
        scratch_shapes=[pltpu.VMEM((PAGED_SLOTS, pg, page_rows, DA_V_DIM), F32),
                        pltpu.VMEM((PAGED_SLOTS, pg, page_rows, DA_V_DIM), F32),
                        pltpu.SemaphoreType.DMA((2, PAGED_SLOTS)),
                        pltpu.VMEM((stat_rows, 1), F32), pltpu.VMEM((stat_rows, 1), F32),
                        pltpu.VMEM((stat_rows, DA_V_DIM), F32)],
    )
    out = pl.pallas_call(
        functools.partial(_paged_body, pg=pg),
        grid_spec=grid_spec,
        out_shape=jax.ShapeDtypeStruct((db, 1, ATTN_WIDTH), BF16),
        compiler_params=_cparams(("arbitrary", "arbitrary")),
        name="attn_paged",
    )(page_table, lam4, g_subln, q.reshape(db, 1, ATTN_WIDTH), k_new.reshape(db, 1, ATTN_WIDTH),
      v_new.reshape(db, 1, ATTN_WIDTH), ck, cv)
    return out.reshape(db, ATTN_WIDTH)


def _gate_norm(y, xs, z, dskip, gssm):
    y = y + dskip * xs
    zf = z.astype(F32)
    yg = y * (zf * _sigmoid(zf))
    gw = SSM_WIDTH // SSM_GROUPS
    outs = []
    for g in range(SSM_GROUPS):
        part = yg[:, g * gw:(g + 1) * gw]
        r = lax.rsqrt(jnp.mean(part * part, axis=-1, keepdims=True) + EPS)
        outs.append(part * r * gssm[:, g * gw:(g + 1) * gw])
    return jnp.concatenate(outs, axis=-1)


def _ssd_body(xbc_ref, z_ref, dt_ref, cw_ref, cb_ref, dtb_ref, alog_ref, dskip_ref, gssm_ref,
              o_ref, st_ref, ext_ref, *, lb):
    j = pl.program_id(1)
    L = SSM_CHUNK
    hpg = SSM_HEADS_PER_GROUP
    gw = hpg * SSM_HEAD_DIM
    pad = 8

    @pl.when(j == 0)
    def _():
        ext_ref[0:pad, :] = jnp.zeros((pad, CONV_CH), F32)
        st_ref[...] = jnp.zeros(st_ref.shape, F32)

    ext_ref[pad:pad + lb, :] = xbc_ref[...]

    cw = cw_ref[...]
    cbias = cb_ref[...]
    a_neg = -jnp.exp(alog_ref[...])
    dtb = dtb_ref[...]
    dskip = dskip_ref[...]
    gssm = gssm_ref[...]
    rr = lax.broadcasted_iota(jnp.int32, (L, L), 0)
    cc = lax.broadcasted_iota(jnp.int32, (L, L), 1)
    causal = cc <= rr
    tri = causal.astype(F32)
    lane_g = lax.broadcasted_iota(jnp.int32, (L, gw), 1) // SSM_HEAD_DIM

    for c in range(lb // L):
        r0 = c * L
        xc = cbias + cw[3:4] * ext_ref[pad + r0:pad + r0 + L, :]
        for k in range(1, SSM_CONV):
            xc = xc + cw[3 - k:4 - k] * ext_ref[pad + r0 - k:pad + r0 - k + L, :]
        xc = xc * _sigmoid(xc)
        xs = xc[:, :SSM_WIDTH]
        dt = _softplus(dt_ref[r0:r0 + L, :] + dtb)
        a = dt * a_neg
        cs = jnp.dot(tri, a, preferred_element_type=F32, precision=lax.Precision.HIGHEST)
        cs_t = cs.T
        dt_t = dt.T
        ecs = jnp.exp(cs)
        wb = jnp.exp(cs[L - 1:L, :] - cs) * dt
        e_last = jnp.exp(cs_t[:, L - 1:L])

        y_parts = []
        for g in range(SSM_GROUPS):
            bg = xc[:, SSM_WIDTH + g * SSM_STATE:SSM_WIDTH + (g + 1) * SSM_STATE].astype(BF16)
            cg = xc[:, SSM_WIDTH + (SSM_GROUPS + g) * SSM_STATE:
                    SSM_WIDTH + (SSM_GROUPS + g + 1) * SSM_STATE].astype(BF16)
            xg = xs[:, g * gw:(g + 1) * gw]
            xg_b = xg.astype(BF16)
            st_g = st_ref[g * gw:(g + 1) * gw, :]
            cbm = lax.dot_general(cg, bg, NT_DIMS, preferred_element_type=F32)
            ys = lax.dot_general(cg, st_g.astype(BF16), NT_DIMS, preferred_element_type=F32)
            yg = jnp.zeros((L, gw), F32)
            ecs_g = jnp.zeros((L, gw), F32)
            wb_g = jnp.zeros((L, gw), F32)
            scale_rows = []
            for r in range(hpg):
                h = g * hpg + r
                seg = cs[:, h:h + 1] - cs_t[h:h + 1, :]
                decay = jnp.exp(jnp.where(causal, seg, -jnp.inf))
                w = (cbm * decay * dt_t[h:h + 1, :]).astype(BF16)
                yh = jnp.dot(w, xg_b, preferred_element_type=F32)
                sel = lane_g == r
                yg = jnp.where(sel, yh, yg)
                ecs_g = jnp.where(sel, ecs[:, h:h + 1], ecs_g)
                wb_g = jnp.where(sel, wb[:, h:h + 1], wb_g)
                scale_rows.append(jnp.broadcast_to(e_last[h:h + 1, :], (SSM_HEAD_DIM, SSM_STATE)))
            y_parts.append(yg + ecs_g * ys)
            xw = (xg * wb_g).astype(BF16)
            upd = lax.dot_general(xw, bg, TN_DIMS, preferred_element_type=F32)
            st_ref[g * gw:(g + 1) * gw, :] = jnp.concatenate(scale_rows, axis=0) * st_g + upd
        y = jnp.concatenate(y_parts, axis=-1)
        o_ref[r0:r0 + L, :] = _gate_norm(y, xs, z_ref[r0:r0 + L, :], dskip, gssm).astype(o_ref.dtype)

    ext_ref[0:pad, :] = ext_ref[lb:lb + pad, :]


def _ssd_prompt(xbc, z, dt, cw, cb, dtb, alog, dskip, gssm, bsz, seq, lb):
    nb = seq // lb
    row = lambda w: pl.BlockSpec((lb, w), lambda b, j: (b * nb + j, 0))
    return pl.pallas_call(
        functools.partial(_ssd_body, lb=lb),
        grid=(bsz, nb),
        in_specs=[row(CONV_CH), row(SSM_WIDTH), row(DT_PAD),
                  _const_spec((SSM_CONV, CONV_CH)), _const_spec((1, CONV_CH)),
                  _const_spec((1, DT_PAD)), _const_spec((1, DT_PAD)),
                  _const_spec((1, SSM_WIDTH)), _const_spec((1, SSM_WIDTH))],
        out_specs=(row(SSM_WIDTH),
                   pl.BlockSpec((None, SSM_WIDTH, SSM_STATE), lambda b, j: (b, 0, 0))),
        out_shape=(jax.ShapeDtypeStruct((bsz * seq, SSM_WIDTH), BF16),
                   jax.ShapeDtypeStruct((bsz, SSM_WIDTH, SSM_STATE), F32)),
        scratch_shapes=[pltpu.VMEM((lb + 8, CONV_CH), F32)],
        compiler_params=_cparams(("parallel", "arbitrary")),
        name="ssd_prompt",
    )(xbc, z, dt, cw, cb, dtb, alog, dskip, gssm)


def _ssd_step_body(xbc_ref, cbuf_ref, z_ref, dt_ref, st_ref, cw_ref, cb_ref, dtb_ref, alog_ref,
                   dskip_ref, gssm_ref, o_ref, sto_ref, y_scr, *, bb):
    hpg = SSM_HEADS_PER_GROUP
    gw = hpg * SSM_HEAD_DIM
    cw = cw_ref[...]
    xc = cb_ref[...] + cw[3:4] * xbc_ref[...]
    for k in range(SSM_CONV - 1):
        xc = xc + cw[k:k + 1] * cbuf_ref[k]
    xc = xc * _sigmoid(xc)
    xs = xc[:, :SSM_WIDTH]
    dt = _softplus(dt_ref[...] + dtb_ref[...])
    da = jnp.exp(dt * (-jnp.exp(alog_ref[...])))
    zpad = jnp.zeros((LANES - bb, LANES), F32)
    xs_t = [jnp.concatenate([xs[:, k * LANES:(k + 1) * LANES], zpad], axis=0).T
            for k in range(SSM_WIDTH // LANES)]
    lane8 = lax.broadcasted_iota(jnp.int32, (8, SSM_WIDTH), 1)
    row8 = lax.broadcasted_iota(jnp.int32, (8, SSM_STATE), 0)

    for b in range(bb):
        new_rows = []
        for h in range(N_SSM_HEADS):
            g = h // hpg
            xt = xs_t[h // 2]
            xcol = xt[(h % 2) * SSM_HEAD_DIM:(h % 2 + 1) * SSM_HEAD_DIM, b:b + 1]
            brow = xc[b:b + 1, SSM_WIDTH + g * SSM_STATE:SSM_WIDTH + (g + 1) * SSM_STATE]
            old = st_ref[b, h * SSM_HEAD_DIM:(h + 1) * SSM_HEAD_DIM, :]
            new = da[b:b + 1, h:h + 1] * old + (dt[b:b + 1, h:h + 1] * xcol) * brow
            sto_ref[b, h * SSM_HEAD_DIM:(h + 1) * SSM_HEAD_DIM, :] = new
            new_rows.append(new.astype(BF16))
        new_all = jnp.concatenate(new_rows, axis=0)
        c0 = SSM_WIDTH + SSM_GROUPS * SSM_STATE
        cmat = jnp.where(row8 == 0, xc[b:b + 1, c0:c0 + SSM_STATE],
                         jnp.where(row8 == 1, xc[b:b + 1, c0 + SSM_STATE:c0 + 2 * SSM_STATE], 0.0))
        yy = lax.dot_general(cmat.astype(BF16), new_all, NT_DIMS, preferred_element_type=F32)
        y_scr[b:b + 1, :] = jnp.where(lane8[0:1] < gw, yy[0:1], yy[1:2])
    o_ref[...] = _gate_norm(y_scr[...], xs, z_ref[...], dskip_ref[...], gssm_ref[...]).astype(o_ref.dtype)


def _ssd_step(xbc, cbuf_t, z, dt, state, cw, cb, dtb, alog, dskip, gssm, bb):
    db = xbc.shape[0]
    row = lambda w: pl.BlockSpec((bb, w), lambda i: (i, 0))
    st = pl.BlockSpec((bb, SSM_WIDTH, SSM_STATE), lambda i: (i, 0, 0))
    return pl.pallas_call(
        functools.partial(_ssd_step_body, bb=bb),
        grid=(db // bb,),
        in_specs=[row(CONV_CH), pl.BlockSpec((SSM_CONV - 1, bb, CONV_CH), lambda i: (0, i, 0)),
                  row(SSM_WIDTH), row(DT_PAD), st,
                  _const_spec((SSM_CONV, CONV_CH)), _const_spec((1, CONV_CH)),
                  _const_spec((1, DT_PAD)), _const_spec((1, DT_PAD)),
                  _const_spec((1, SSM_WIDTH)), _const_spec((1, SSM_WIDTH))],
        out_specs=(row(SSM_WIDTH), st),
        out_shape=(jax.ShapeDtypeStruct((db, SSM_WIDTH), BF16),
                   jax.ShapeDtypeStruct((db, SSM_WIDTH, SSM_STATE), F32)),
        scratch_shapes=[pltpu.VMEM((bb, SSM_WIDTH), F32)],
        compiler_params=_cparams(("parallel",)),
        name="ssd_step",
    )(xbc, cbuf_t, z, dt, state, cw, cb, dtb, alog, dskip, gssm)


FFN_CHUNK = D_FF


def _ffn_body(*refs, tm, sample):
    if sample:
        (x_ref, oa_ref, os_ref, wo_ref, gf_ref, wu_ref, cw_ref, cb_ref, wd_ref, gfin_ref,
         p2_ref, p1_ref, y_ref, up_ref) = refs
    else:
        (x_ref, oa_ref, os_ref, wo_ref, gf_ref, wu_ref, cw_ref, cb_ref, wd_ref, gfin_ref,
         y_ref, carry_ref, *ext_refs) = refs
        pad = 8

        @pl.when(pl.program_id(1) == 0)
        def _():
            carry_ref[...] = jnp.zeros(carry_ref.shape, F32)

    x1 = (x_ref[...]
          + jnp.dot(oa_ref[...], wo_ref[0:ATTN_WIDTH, :], preferred_element_type=F32)
          + jnp.dot(os_ref[...], wo_ref[ATTN_WIDTH:, :], preferred_element_type=F32))
    r = lax.rsqrt(jnp.mean(x1 * x1, axis=-1, keepdims=True) + EPS)
    h2 = (x1 * r * gf_ref[...]).astype(BF16)
    cw = cw_ref[...]
    cbias = cb_ref[...]

    n_chunks = D_FF // FFN_CHUNK

    def col_ranges(c):
        return [(lo, lo + FFN_CHUNK) for lo in (c * FFN_CHUNK, D_FF + c * FFN_CHUNK)]

    def up_proj(c):
        return [jnp.dot(h2, wu_ref[:, lo:hi], preferred_element_type=F32) for lo, hi in col_ranges(c)]

    acc = jnp.zeros_like(x1)
    u_next = up_proj(0)
    for c in range(n_chunks):
        u_cur = u_next
        if c + 1 < n_chunks:
            u_next = up_proj(c + 1)
        parts = []
        for k, ((lo, hi), u) in enumerate(zip(col_ranges(c), u_cur)):
            if sample:
                up_ref[:, lo:hi] = u
                u1 = p1_ref[:, lo:hi]
                u2 = p2_ref[:, lo:hi]
            else:
                ext = ext_refs[2 * (c % 2) + k]
                ext[0:pad, :] = carry_ref[:, lo:hi]
                ext[pad:pad + tm, :] = u
                u1 = ext[pad - 1:pad - 1 + tm, :]
                u2 = ext[pad - 2:pad - 2 + tm, :]
                carry_ref[:, lo:hi] = ext[tm:tm + pad, :]
            parts.append(cbias[:, lo:hi] + cw[0:1, lo:hi] * u2 + cw[1:2, lo:hi] * u1 + cw[2:3, lo:hi] * u)
        gate, val = parts
        act = ((gate * _sigmoid(gate)) * val).astype(BF16)
        acc = acc + jnp.dot(act, wd_ref[c * FFN_CHUNK:(c + 1) * FFN_CHUNK, :], preferred_element_type=F32)
    x2 = x1 + acc
    r2 = lax.rsqrt(jnp.mean(x2 * x2, axis=-1, keepdims=True) + EPS)
    y_ref[...] = x2 * r2 * gfin_ref[...]


def _ffn_weights_specs():
    return [_const_spec((D_MODEL, D_MODEL)), _const_spec((1, D_MODEL)),
            _const_spec((D_MODEL, 2 * D_FF)), _const_spec((FFN_CONV, 2 * D_FF)),
            _const_spec((1, 2 * D_FF)), _const_spec((D_FF, D_MODEL)), _const_spec((1, D_MODEL))]


def _ffn_prompt(x2d, oa, os_, weights, bsz, seq, tm):
    nt = seq // tm
    row = lambda w: pl.BlockSpec((tm, w), lambda b, i: (b * nt + i, 0))
    return pl.pallas_call(
        functools.partial(_ffn_body, tm=tm, sample=False),
        grid=(bsz, nt),
        in_specs=[row(D_MODEL), row(ATTN_WIDTH), row(SSM_WIDTH)] + _ffn_weights_specs(),
        out_specs=(row(D_MODEL), pl.BlockSpec((None, 8, 2 * D_FF), lambda b, i: (b, 0, 0))),
        out_shape=(jax.ShapeDtypeStruct((bsz * seq, D_MODEL), F32),
                   jax.ShapeDtypeStruct((bsz, 8, 2 * D_FF), F32)),
        scratch_shapes=[pltpu.VMEM((tm + 8, FFN_CHUNK), F32)] * (2 * min(2, D_FF // FFN_CHUNK)),
        compiler_params=_cparams(("parallel", "arbitrary")),
        name="ffn_prompt",
    )(x2d, oa, os_, *weights)


def _ffn_sample(x2d, oa, os_, weights, prev2, prev1):
    db = x2d.shape[0]
    full = lambda w: pl.BlockSpec((db, w), lambda i: (0, 0))
    return pl.pallas_call(
        functools.partial(_ffn_body, tm=db, sample=True),
        grid=(1,),
        in_specs=[full(D_MODEL), full(ATTN_WIDTH), full(SSM_WIDTH)] + _ffn_weights_specs()
                 + [full(2 * D_FF), full(2 * D_FF)],
        out_specs=(full(D_MODEL), full(2 * D_FF)),
        out_shape=(jax.ShapeDtypeStruct((db, D_MODEL), F32),
                   jax.ShapeDtypeStruct((db, 2 * D_FF), F32)),
        compiler_params=_cparams(("arbitrary",)),
        name="ffn_sample",
    )(x2d, oa, os_, *weights, prev2, prev1)


def _pick(pref, n):
    t = min(pref, n)
    assert n % t == 0, (pref, n)
    return t


def kernel(x_prompt, x_sample, cache_k, cache_v, page_table, state_ssm, state_ssm_conv,
           state_ffn_conv, g_mix, w_in, lam_q1, lam_k1, lam_q2, lam_k2, g_subln,
           ssm_conv_w, ssm_conv_b, dt_bias, A_log, D_skip, g_ssm, w_out, g_ffn, w_up,
           ffn_conv_w, ffn_conv_b, w_down, g_final):
    bsz, seq, _ = x_prompt.shape
    db = x_sample.shape[0]
    assert x_sample.shape[1] == 1 and w_in.shape[0] == 1

    in_cols = w_in.shape[-1]
    w_in_b = jnp.pad(w_in[0], ((0, 0), (0, IN_COLS_PAD - in_cols))).astype(BF16)
    lam4 = jnp.concatenate([lam_q1, lam_k1, lam_q2, lam_k2], axis=0)
    head_pad = lambda v: jnp.pad(v, ((0, 0), (0, DT_PAD - N_SSM_HEADS)))
    dtb, alog = head_pad(dt_bias), head_pad(A_log)
    dskip = jnp.repeat(D_skip, SSM_HEAD_DIM, axis=-1)
    ssd_params = (ssm_conv_w[0], ssm_conv_b, dtb, alog, dskip, g_ssm)
    ffn_weights = (w_out[0].astype(BF16), g_ffn, w_up[0].astype(BF16), ffn_conv_w[0], ffn_conv_b,
                   w_down[0].astype(BF16), g_final.reshape(1, D_MODEL))

    n = bsz * seq
    xp = x_prompt.reshape(n, D_MODEL)
    q, k, v, kb, vb, z, xbc, dt = _in_proj(xp, g_mix, w_in_b, _pick(512, n))
    o_attn = _attn_prompt(q, kb, vb, lam4, g_subln, bsz, seq, _pick(1024, seq))
    o_ssm, st_p = _ssd_prompt(xbc, z, dt, *ssd_params, bsz, seq, _pick(512, seq))
    y_p, ffn_carry = _ffn_prompt(xp, o_attn, o_ssm, ffn_weights, bsz, seq, _pick(512, seq))

    kv_shape = (1, bsz, seq, N_ATTN_HEADS, DA_V_DIM)
    outs_p = (y_p.reshape(bsz, seq, D_MODEL), k.reshape(kv_shape), v.reshape(kv_shape),
              st_p.reshape(1, bsz, N_SSM_HEADS, SSM_HEAD_DIM, SSM_STATE),
              xbc.reshape(bsz, seq, CONV_CH)[None, :, seq - (SSM_CONV - 1):],
              ffn_carry[None, :, 8 - (FFN_CONV - 1):])

    xs = x_sample.reshape(db, D_MODEL)
    q_s, k_s, v_s, _, _, z_s, xbc_s, dt_s = _in_proj(xs, g_mix, w_in_b, db)
    n_pages = page_table.shape[1]
    o_attn_s = _attn_paged(page_table, lam4, g_subln, q_s, k_s, v_s, cache_k, cache_v,
                           _pick(16, n_pages))
    cbuf_t = jnp.swapaxes(state_ssm_conv[0], 0, 1)
    o_ssm_s, st_s = _ssd_step(xbc_s, cbuf_t, z_s, dt_s,
                              state_ssm[0].reshape(db, SSM_WIDTH, SSM_STATE), *ssd_params, _pick(8, db))
    y_s, up_s = _ffn_sample(xs, o_attn_s, o_ssm_s, ffn_weights,
                            state_ffn_conv[0, :, 0], state_ffn_conv[0, :, 1])

    kv_shape_s = (1, db, 1, N_ATTN_HEADS, DA_V_DIM)
    outs_s = (y_s.reshape(db, 1, D_MODEL), k_s.reshape(kv_shape_s), v_s.reshape(kv_shape_s),
              st_s.reshape(1, db, N_SSM_HEADS, SSM_HEAD_DIM, SSM_STATE),
              jnp.concatenate([state_ssm_conv[:, :, 1:], xbc_s[None, :, None, :]], axis=2),
              jnp.concatenate([state_ffn_conv[:, :, 1:], up_s[None, :, None, :]], axis=2))

    return (outs_p[0], outs_s[0]) + outs_p[1:] + outs_s[1:]
```

```python
import functools
import math

import jax
import jax.numpy as jnp
from jax import lax
from jax.experimental import pallas as pl
from jax.experimental.pallas import tpu as pltpu

F32 = jnp.float32
BF16 = jnp.bfloat16

EPS = 1e-5
D_MODEL = 1024
ATTN_WIDTH = 512
DA_HEAD_DIM = 64
DA_V_DIM = 128
N_ATTN_HEADS = 4
SSM_WIDTH = 512
SSM_HEAD_DIM = 64
N_SSM_HEADS = 8
SSM_GROUPS = 2
SSM_HEADS_PER_GROUP = 4
SSM_STATE = 128
SSM_CONV = 4
SSM_CHUNK = 128
CONV_CH = 1024
D_FF = 2816
FFN_CONV = 3
PAGE_SIZE = 128
LANES = 128
DT_PAD = LANES
IN_COLS_PAD = 3 * ATTN_WIDTH + SSM_WIDTH + CONV_CH + DT_PAD
QK_SCALE = DA_HEAD_DIM ** -0.5 * math.log2(math.e)
LAM0 = 0.8 - 0.6 * math.exp(-0.3 * 0)
VMEM_LIMIT = 56 * 1024 * 1024

NT_DIMS = (((1,), (1,)), ((), ()))
TN_DIMS = (((0,), (0,)), ((), ()))


def _cparams(sem):
    return pltpu.CompilerParams(dimension_semantics=sem, vmem_limit_bytes=VMEM_LIMIT)


def _const_spec(shape):
    nd = len(shape)
    return pl.BlockSpec(shape, lambda *_: (0,) * nd, pipeline_mode=pl.Buffered(1))


def _sigmoid(x):
    return 1.0 / (1.0 + jnp.exp(-x))


def _softplus(x):
    return jnp.maximum(x, 0.0) + jnp.log1p(jnp.exp(-jnp.abs(x)))


def _lam_from_params(lam_ref):
    lp = lam_ref[...]
    t1 = jnp.sum(lp[0:1] * lp[1:2], axis=-1, keepdims=True)
    t2 = jnp.sum(lp[2:3] * lp[3:4], axis=-1, keepdims=True)
    return jnp.exp(t1) - jnp.exp(t2) + LAM0


def _inproj_body(x_ref, g_ref, w_ref, q_ref, k_ref, v_ref, kb_ref, vb_ref, z_ref, xbc_ref, dt_ref):
    x = x_ref[...]
    r = lax.rsqrt(jnp.mean(x * x, axis=-1, keepdims=True) + EPS)
    h = (x * r * g_ref[...]).astype(BF16)

    def proj(lo, hi):
        return jnp.dot(h, w_ref[:, lo:hi], preferred_element_type=F32)

    a = ATTN_WIDTH
    q_ref[...] = (proj(0, a) * QK_SCALE).astype(BF16)
    k = proj(a, 2 * a)
    kb_ref[...] = k.astype(BF16)
    v = proj(2 * a, 3 * a)
    for head in range(N_ATTN_HEADS):
        vb_ref[head * DA_V_DIM:(head + 1) * DA_V_DIM, :] = v[:, head * DA_V_DIM:(head + 1) * DA_V_DIM].T.astype(BF16)
    tm = x.shape[0]
    for head in range(N_ATTN_HEADS):
        cols = slice(head * DA_V_DIM, (head + 1) * DA_V_DIM)
        k_ref[pl.ds(head, tm, stride=N_ATTN_HEADS), :] = k[:, cols]
        v_ref[pl.ds(head, tm, stride=N_ATTN_HEADS), :] = v[:, cols]
    z_ref[...] = proj(3 * a, 3 * a + SSM_WIDTH).astype(BF16)
    c0 = 3 * a + SSM_WIDTH
    xbc_ref[...] = proj(c0, c0 + CONV_CH)
    dt_ref[...] = proj(c0 + CONV_CH, c0 + CONV_CH + DT_PAD)


def _in_proj(x2d, g_mix, w_in_b, tm):
    n = x2d.shape[0]
    a = ATTN_WIDTH
    row = lambda w: pl.BlockSpec((tm, w), lambda i: (i, 0))
    out_shape = (
        jax.ShapeDtypeStruct((n, a), BF16),
        jax.ShapeDtypeStruct((n * N_ATTN_HEADS, DA_V_DIM), F32),
        jax.ShapeDtypeStruct((n * N_ATTN_HEADS, DA_V_DIM), F32),
        jax.ShapeDtypeStruct((n, a), BF16),
        jax.ShapeDtypeStruct((n // tm * a, tm), BF16),
        jax.ShapeDtypeStruct((n, SSM_WIDTH), BF16),
        jax.ShapeDtypeStruct((n, CONV_CH), F32),
        jax.ShapeDtypeStruct((n, DT_PAD), F32),
    )
    return pl.pallas_call(
        _inproj_body,
        grid=(n // tm,),
        in_specs=[row(D_MODEL), _const_spec((1, D_MODEL)), _const_spec((D_MODEL, IN_COLS_PAD))],
        out_specs=(row(a), pl.BlockSpec((tm * N_ATTN_HEADS, DA_V_DIM), lambda i: (i, 0)),
                   pl.BlockSpec((tm * N_ATTN_HEADS, DA_V_DIM), lambda i: (i, 0)),
                   row(a), pl.BlockSpec((a, tm), lambda i: (i, 0)),
                   row(SSM_WIDTH), row(CONV_CH), row(DT_PAD)),
        out_shape=out_shape,
        compiler_params=_cparams(("parallel",)),
        name="in_proj",
    )(x2d, g_mix, w_in_b)


def _attn_body(lam_ref, gsub_ref, q_ref, k_ref, v_ref, o_ref, *, t, tv):
    i = pl.program_id(2)
    q = q_ref[...]
    lane = lax.broadcasted_iota(jnp.int32, q.shape, 1)
    zero = jnp.zeros_like(q)
    q1 = jnp.where(lane < DA_HEAD_DIM, q, zero)
    q2 = jnp.where(lane >= DA_HEAD_DIM, q, zero)

    def update(s, vt, m, l, acc):
        m_new = jnp.maximum(m, jnp.max(s, axis=0, keepdims=True))
        alpha = jnp.exp2(m - m_new)
        p = jnp.exp2(s - m_new)
        l = alpha * l + jnp.sum(p, axis=0, keepdims=True)
        acc = alpha * acc + jnp.dot(vt, p.astype(BF16), preferred_element_type=F32)
        return m_new, l, acc

    def scores(j, masked):
        start = pl.multiple_of(j * t, t)
        kt = k_ref[pl.ds(start, t), :]
        s1 = lax.dot_general(kt, q1, NT_DIMS, preferred_element_type=F32)
        s2 = lax.dot_general(kt, q2, NT_DIMS, preferred_element_type=F32)
        if masked:
            kk = lax.broadcasted_iota(jnp.int32, (t, t), 0)
            qq = lax.broadcasted_iota(jnp.int32, (t, t), 1)
            keep = kk <= qq
            s1 = jnp.where(keep, s1, -jnp.inf)
            s2 = jnp.where(keep, s2, -jnp.inf)
        per = t // tv
        vt = jnp.concatenate([v_ref[j * per + c] for c in range(per)], axis=1)
        return s1, s2, vt

    def tiles(js, carry, masked):
        m1, l1, a1, m2, l2, a2 = carry
        for j in js:
            s1, s2, vt = scores(j, masked)
            m1, l1, a1 = update(s1, vt, m1, l1, a1)
            m2, l2, a2 = update(s2, vt, m2, l2, a2)
        return m1, l1, a1, m2, l2, a2

    ninf = jnp.full((1, t), -jnp.inf, F32)
    zl = jnp.zeros((1, t), F32)
    za = jnp.zeros((DA_V_DIM, t), F32)
    carry = (ninf, zl, za, ninf, zl, za)
    carry = lax.fori_loop(0, i // 2, lambda jj, c: tiles((2 * jj, 2 * jj + 1), c, False), carry)
    carry = lax.cond(i % 2 == 1, lambda c: tiles((i - 1,), c, False), lambda c: c, carry)
    m1, l1, a1, m2, l2, a2 = tiles((i,), carry, True)

    lam = _lam_from_params(lam_ref)
    o = a1 / l1 - lam * (a2 / l2)
    r = lax.rsqrt(jnp.mean(o * o, axis=0, keepdims=True) + EPS)
    o = (o * r * gsub_ref[...]) * (1.0 - LAM0)
    o_ref[...] = o.T.astype(o_ref.dtype)


def _attn_prompt(q, kb, vbt, lam4, g_subln, bsz, seq, t, tv):
    nq = seq // t
    nv = seq // tv
    hb = pl.BlockSpec((t, DA_V_DIM), lambda b, h, i: (b * nq + i, h))
    kv = pl.BlockSpec((seq, DA_V_DIM), lambda b, h, i: (b, h))
    v4 = vbt.reshape(bsz, nv, N_ATTN_HEADS, DA_V_DIM, tv)
    vspec = pl.BlockSpec((None, nv, None, DA_V_DIM, tv), lambda b, h, i: (b, 0, h, 0, 0))
    return pl.pallas_call(
        functools.partial(_attn_body, t=t, tv=tv),
        grid=(bsz, N_ATTN_HEADS, nq),
        in_specs=[_const_spec((4, DA_HEAD_DIM)), _const_spec((DA_V_DIM, 1)), hb, kv, vspec],
        out_specs=hb,
        out_shape=jax.ShapeDtypeStruct((bsz * seq, ATTN_WIDTH), BF16),
        compiler_params=_cparams(("parallel", "parallel", "arbitrary")),
        name="attn_prompt",
    )(lam4, g_subln.reshape(DA_V_DIM, 1), q, kb, v4)


PAGED_ROWS = 16
PAGED_SLOTS = 3


def _paged_body(pt_ref, lam_ref, gsub_ref, q_ref, kn_ref, vn_ref, ck_hbm, cv_hbm, o_ref,
                kbuf, vbuf, sems, m_scr, l_scr, acc_scr, *, pg):
    b = pl.program_id(0)
    g = pl.program_id(1)
    n_groups = pl.num_programs(1)
    step = b * n_groups + g
    n_steps = pl.num_programs(0) * n_groups
    rows = PAGED_ROWS
    nh = N_ATTN_HEADS
    page_rows = PAGE_SIZE * nh

    def group_copies(st, slot):
        bb = st // n_groups
        gg = st % n_groups
        out = []
        for i in range(pg):
            row0 = pl.multiple_of(pt_ref[bb, gg * pg + i] * page_rows, page_rows)
            out.append(pltpu.make_async_copy(ck_hbm.at[pl.ds(row0, page_rows), :], kbuf.at[slot, i], sems.at[0, slot]))
            out.append(pltpu.make_async_copy(cv_hbm.at[pl.ds(row0, page_rows), :], vbuf.at[slot, i], sems.at[1, slot]))
        return out

    @pl.when(step == 0)
    def _():
        for st in range(PAGED_SLOTS - 1):
            @pl.when(st < n_steps)
            def _():
                for cp in group_copies(st, st):
                    cp.start()

    ahead = step + (PAGED_SLOTS - 1)

    @pl.when(ahead < n_steps)
    def _():
        for cp in group_copies(ahead, ahead % PAGED_SLOTS):
            cp.start()

    slot = step % PAGED_SLOTS
    for cp in group_copies(step, slot):
        cp.wait()

    row = lax.broadcasted_iota(jnp.int32, (rows, DA_V_DIM), 0)
    lane = lax.broadcasted_iota(jnp.int32, (rows, DA_V_DIM), 1)
    comp = lane // DA_HEAD_DIM == row
    qf = q_ref[0].astype(F32)

    def qmat(h):
        qh = qf[:, h * DA_V_DIM:(h + 1) * DA_V_DIM]
        return jnp.where(comp, jnp.broadcast_to(qh, (rows, DA_V_DIM)), 0.0)

    @pl.when(g == 0)
    def _():
        m_scr[...] = jnp.full(m_scr.shape, -jnp.inf, F32)
        l_scr[...] = jnp.zeros(l_scr.shape, F32)
        acc_scr[...] = jnp.zeros(acc_scr.shape, F32)

    m_all, l_all, acc_all = m_scr[...], l_scr[...], acc_scr[...]
    m_out, l_out, acc_out = [], [], []
    scores = []
    for h in range(nh):
        qm_b = qmat(h).astype(BF16)
        s_list = []
        for i in range(pg):
            kh = kbuf[slot, i, pl.ds(h, PAGE_SIZE, stride=nh), :].astype(BF16)
            s_list.append(lax.dot_general(qm_b, kh, NT_DIMS, preferred_element_type=F32))
        scores.append(s_list)
    probs = []
    for h in range(nh):
        hs = slice(h * rows, (h + 1) * rows)
        s_list = scores[h]
        s_max = s_list[0]
        for s in s_list[1:]:
            s_max = jnp.maximum(s_max, s)
        m_old = m_all[hs, :]
        m_new = jnp.maximum(m_old, jnp.max(s_max, axis=-1, keepdims=True))
        alpha = jnp.exp2(m_old - m_new)
        p_list = [jnp.exp2(s - m_new) for s in s_list]
        l_add = p_list[0]
        for p in p_list[1:]:
            l_add = l_add + p
        m_out.append(m_new)
        l_out.append(alpha * l_all[hs, :] + jnp.sum(l_add, axis=-1, keepdims=True))
        probs.append((alpha, [p.astype(BF16) for p in p_list]))
    for h in range(nh):
        hs = slice(h * rows, (h + 1) * rows)
        alpha, p_list = probs[h]
        pv = None
        for i, p in enumerate(p_list):
            vh = vbuf[slot, i, pl.ds(h, PAGE_SIZE, stride=nh), :].astype(BF16)
            d = jnp.dot(p, vh, preferred_element_type=F32)
            pv = d if pv is None else pv + d
        acc_out.append(alpha * acc_all[hs, :] + pv)
    m_scr[...] = jnp.concatenate(m_out, axis=0)
    l_scr[...] = jnp.concatenate(l_out, axis=0)
    acc_scr[...] = jnp.concatenate(acc_out, axis=0)

    @pl.when(g == n_groups - 1)
    def _():
        lam = _lam_from_params(lam_ref)
        gs = gsub_ref[...]
        for h in range(nh):
            hs = slice(h * rows, (h + 1) * rows)
            cols = slice(h * DA_V_DIM, (h + 1) * DA_V_DIM)
            s_self = jnp.sum(qmat(h) * kn_ref[0][:, cols], axis=-1, keepdims=True)
            m_old = m_scr[hs, :]
            m_fin = jnp.maximum(m_old, s_self)
            a_fin = jnp.exp2(m_old - m_fin)
            p_self = jnp.exp2(s_self - m_fin)
            l_fin = a_fin * l_scr[hs, :] + p_self
            acc = (a_fin * acc_scr[hs, :] + p_self * vn_ref[0][:, cols]) / l_fin
            oh = acc[0:1] - lam * acc[1:2]
            r = lax.rsqrt(jnp.mean(oh * oh, axis=-1, keepdims=True) + EPS)
            o_ref[0, :, cols] = ((oh * r * gs) * (1.0 - LAM0)).astype(o_ref.dtype)


def _attn_paged(page_table, lam4, g_subln, q, k_new, v_new, cache_k, cache_v, pg):
    db, n_pages = page_table.shape
    n_groups = n_pages // pg
    page_rows = PAGE_SIZE * N_ATTN_HEADS
    ck = cache_k.reshape(-1, DA_V_DIM)
    cv = cache_v.reshape(-1, DA_V_DIM)
    tok = pl.BlockSpec((1, 1, ATTN_WIDTH), lambda b, g, pt: (b, 0, 0))
    hbm = pl.BlockSpec(memory_space=pl.ANY)
    stat_rows = N_ATTN_HEADS * PAGED_ROWS
    grid_spec = pltpu.PrefetchScalarGridSpec(
        num_scalar_prefetch=1,
        grid=(db, n_groups),
        in_specs=[pl.BlockSpec((4, DA_HEAD_DIM), lambda b, g, pt: (0, 0)),
                  pl.BlockSpec((1, DA_V_DIM), lambda b, g, pt: (0, 0)),
                  tok, tok, tok, hbm, hbm],
        out_specs=tok,
        scratch_shapes=[pltpu.VMEM((PAGED_SLOTS, pg, page_rows, DA_V_DIM), F32),
                        pltpu.VMEM((PAGED_SLOTS, pg, page_rows, DA_V_DIM), F32),
                        pltpu.SemaphoreType.DMA((2, PAGED_SLOTS)),
                        pltpu.VMEM((stat_rows, 1), F32), pltpu.VMEM((stat_rows, 1), F32),
                        pltpu.VMEM((stat_rows, DA_V_DIM), F32)],
    )
    out = pl.pallas_call(
        functools.partial(_paged_body, pg=pg),
        grid_spec=grid_spec,
        out_shape=jax.ShapeDtypeStruct((db, 1, ATTN_WIDTH), BF16),
        compiler_params=_cparams(("arbitrary", "arbitrary")),
        name="attn_paged",
    )(page_table, lam4, g_subln, q.reshape(db, 1, ATTN_WIDTH), k_new.reshape(db, 1, ATTN_WIDTH),
      v_new.reshape(db, 1, ATTN_WIDTH), ck, cv)
    return out.reshape(db, ATTN_WIDTH)


def _gate_norm(y, xs, z, dskip, gssm):
    y = y + dskip * xs
    zf = z.astype(F32)
    yg = y * (zf * _sigmoid(zf))
    gw = SSM_WIDTH // SSM_GROUPS
    outs = []
    for g in range(SSM_GROUPS):
        part = yg[:, g * gw:(g + 1) * gw]
        r = lax.rsqrt(jnp.mean(part * part, axis=-1, keepdims=True) + EPS)
        outs.append(part * r * gssm[:, g * gw:(g + 1) * gw])
    return jnp.concatenate(outs, axis=-1)


def _ssd_body(xbc_ref, z_ref, dt_ref, cw_ref, cb_ref, dtb_ref, alog_ref, dskip_ref, gssm_ref,
              o_ref, st_ref, ext_ref, *, lb):
    j = pl.program_id(1)
    L = SSM_CHUNK
    hpg = SSM_HEADS_PER_GROUP
    gw = hpg * SSM_HEAD_DIM
    pad = 8

    @pl.when(j == 0)
    def _():
        ext_ref[0:pad, :] = jnp.zeros((pad, CONV_CH), F32)
        st_ref[...] = jnp.zeros(st_ref.shape, F32)

    ext_ref[pad:pad + lb, :] = xbc_ref[...]

    cw = cw_ref[...]
    cbias = cb_ref[...]
    a_neg = -jnp.exp(alog_ref[...])
    dtb = dtb_ref[...]
    dskip = dskip_ref[...]
    gssm = gssm_ref[...]
    rr = lax.broadcasted_iota(jnp.int32, (L, L), 0)
    cc = lax.broadcasted_iota(jnp.int32, (L, L), 1)
    causal = cc <= rr
    tri = causal.astype(F32)
    lane_g = lax.broadcasted_iota(jnp.int32, (L, gw), 1) // SSM_HEAD_DIM

    for c in range(lb // L):
        r0 = c * L
        xc = cbias + cw[3:4] * ext_ref[pad + r0:pad + r0 + L, :]
        for k in range(1, SSM_CONV):
            xc = xc + cw[3 - k:4 - k] * ext_ref[pad + r0 - k:pad + r0 - k + L, :]
        xc = xc * _sigmoid(xc)
        xs = xc[:, :SSM_WIDTH]
        dt = _softplus(dt_ref[r0:r0 + L, :] + dtb)
        a = dt * a_neg
        cs = jnp.dot(tri, a, preferred_element_type=F32, precision=lax.Precision.HIGHEST)
        cs_t = cs.T
        dt_t = dt.T
        ecs = jnp.exp(cs)
        wb = jnp.exp(cs[L - 1:L, :] - cs) * dt
        e_last = jnp.exp(cs_t[:, L - 1:L])

        y_parts = []
        for g in range(SSM_GROUPS):
            bg = xc[:, SSM_WIDTH + g * SSM_STATE:SSM_WIDTH + (g + 1) * SSM_STATE].astype(BF16)
            cg = xc[:, SSM_WIDTH + (SSM_GROUPS + g) * SSM_STATE:
                    SSM_WIDTH + (SSM_GROUPS + g + 1) * SSM_STATE].astype(BF16)
            xg = xs[:, g * gw:(g + 1) * gw]
            xg_b = xg.astype(BF16)
            st_g = st_ref[g * gw:(g + 1) * gw, :]
            cbm = lax.dot_general(cg, bg, NT_DIMS, preferred_element_type=F32)
            ys = lax.dot_general(cg, st_g.astype(BF16), NT_DIMS, preferred_element_type=F32)
            yg = jnp.zeros((L, gw), F32)
            ecs_g = jnp.zeros((L, gw), F32)
            wb_g = jnp.zeros((L, gw), F32)
            scale_rows = []
            for r in range(hpg):
                h = g * hpg + r
                seg = cs[:, h:h + 1] - cs_t[h:h + 1, :]
                decay = jnp.exp(jnp.where(causal, seg, -jnp.inf))
                w = (cbm * decay * dt_t[h:h + 1, :]).astype(BF16)
                yh = jnp.dot(w, xg_b, preferred_element_type=F32)
                sel = lane_g == r
                yg = jnp.where(sel, yh, yg)
                ecs_g = jnp.where(sel, ecs[:, h:h + 1], ecs_g)
                wb_g = jnp.where(sel, wb[:, h:h + 1], wb_g)
                scale_rows.append(jnp.broadcast_to(e_last[h:h + 1, :], (SSM_HEAD_DIM, SSM_STATE)))
            y_parts.append(yg + ecs_g * ys)
            xw = (xg * wb_g).astype(BF16)
            upd = lax.dot_general(xw, bg, TN_DIMS, preferred_element_type=F32)
            st_ref[g * gw:(g + 1) * gw, :] = jnp.concatenate(scale_rows, axis=0) * st_g + upd
        y = jnp.concatenate(y_parts, axis=-1)
        o_ref[r0:r0 + L, :] = _gate_norm(y, xs, z_ref[r0:r0 + L, :], dskip, gssm).astype(o_ref.dtype)

    ext_ref[0:pad, :] = ext_ref[lb:lb + pad, :]


def _ssd_prompt(xbc, z, dt, cw, cb, dtb, alog, dskip, gssm, bsz, seq, lb):
    nb = seq // lb
    row = lambda w: pl.BlockSpec((lb, w), lambda b, j: (b * nb + j, 0))
    return pl.pallas_call(
        functools.partial(_ssd_body, lb=lb),
        grid=(bsz, nb),
        in_specs=[row(CONV_CH), row(SSM_WIDTH), row(DT_PAD),
                  _const_spec((SSM_CONV, CONV_CH)), _const_spec((1, CONV_CH)),
                  _const_spec((1, DT_PAD)), _const_spec((1, DT_PAD)),
                  _const_spec((1, SSM_WIDTH)), _const_spec((1, SSM_WIDTH))],
        out_specs=(row(SSM_WIDTH),
                   pl.BlockSpec((None, SSM_WIDTH, SSM_STATE), lambda b, j: (b, 0, 0))),
        out_shape=(jax.ShapeDtypeStruct((bsz * seq, SSM_WIDTH), BF16),
                   jax.ShapeDtypeStruct((bsz, SSM_WIDTH, SSM_STATE), F32)),
        scratch_shapes=[pltpu.VMEM((lb + 8, CONV_CH), F32)],
        compiler_params=_cparams(("parallel", "arbitrary")),
        name="ssd_prompt",
    )(xbc, z, dt, cw, cb, dtb, alog, dskip, gssm)


def _ssd_step_body(xbc_ref, cbuf_ref, z_ref, dt_ref, st_ref, cw_ref, cb_ref, dtb_ref, alog_ref,
                   dskip_ref, gssm_ref, o_ref, sto_ref, y_scr, *, bb):
    hpg = SSM_HEADS_PER_GROUP
    gw = hpg * SSM_HEAD_DIM
    cw = cw_ref[...]
    xc = cb_ref[...] + cw[3:4] * xbc_ref[...]
    for k in range(SSM_CONV - 1):
        xc = xc + cw[k:k + 1] * cbuf_ref[k]
    xc = xc * _sigmoid(xc)
    xs = xc[:, :SSM_WIDTH]
    dt = _softplus(dt_ref[...] + dtb_ref[...])
    da = jnp.exp(dt * (-jnp.exp(alog_ref[...])))
    zpad = jnp.zeros((LANES - bb, LANES), F32)
    xs_t = [jnp.concatenate([xs[:, k * LANES:(k + 1) * LANES], zpad], axis=0).T
            for k in range(SSM_WIDTH // LANES)]
    lane8 = lax.broadcasted_iota(jnp.int32, (8, SSM_WIDTH), 1)
    row8 = lax.broadcasted_iota(jnp.int32, (8, SSM_STATE), 0)

    for b in range(bb):
        new_rows = []
        for h in range(N_SSM_HEADS):
            g = h // hpg
            xt = xs_t[h // 2]
            xcol = xt[(h % 2) * SSM_HEAD_DIM:(h % 2 + 1) * SSM_HEAD_DIM, b:b + 1]
            brow = xc[b:b + 1, SSM_WIDTH + g * SSM_STATE:SSM_WIDTH + (g + 1) * SSM_STATE]
            old = st_ref[b, h * SSM_HEAD_DIM:(h + 1) * SSM_HEAD_DIM, :]
            new = da[b:b + 1, h:h + 1] * old + (dt[b:b + 1, h:h + 1] * xcol) * brow
            sto_ref[b, h * SSM_HEAD_DIM:(h + 1) * SSM_HEAD_DIM, :] = new
            new_rows.append(new.astype(BF16))
        new_all = jnp.concatenate(new_rows, axis=0)
        c0 = SSM_WIDTH + SSM_GROUPS * SSM_STATE
        cmat = jnp.where(row8 == 0, xc[b:b + 1, c0:c0 + SSM_STATE],
                         jnp.where(row8 == 1, xc[b:b + 1, c0 + SSM_STATE:c0 + 2 * SSM_STATE], 0.0))
        yy = lax.dot_general(cmat.astype(BF16), new_all, NT_DIMS, preferred_element_type=F32)
        y_scr[b:b + 1, :] = jnp.where(lane8[0:1] < gw, yy[0:1], yy[1:2])
    o_ref[...] = _gate_norm(y_scr[...], xs, z_ref[...], dskip_ref[...], gssm_ref[...]).astype(o_ref.dtype)


def _ssd_step(xbc, cbuf_t, z, dt, state, cw, cb, dtb, alog, dskip, gssm, bb):
    db = xbc.shape[0]
    row = lambda w: pl.BlockSpec((bb, w), lambda i: (i, 0))
    st = pl.BlockSpec((bb, SSM_WIDTH, SSM_STATE), lambda i: (i, 0, 0))
    return pl.pallas_call(
        functools.partial(_ssd_step_body, bb=bb),
        grid=(db // bb,),
        in_specs=[row(CONV_CH), pl.BlockSpec((SSM_CONV - 1, bb, CONV_CH), lambda i: (0, i, 0)),
                  row(SSM_WIDTH), row(DT_PAD), st,
                  _const_spec((SSM_CONV, CONV_CH)), _const_spec((1, CONV_CH)),
                  _const_spec((1, DT_PAD)), _const_spec((1, DT_PAD)),
                  _const_spec((1, SSM_WIDTH)), _const_spec((1, SSM_WIDTH))],
        out_specs=(row(SSM_WIDTH), st),
        out_shape=(jax.ShapeDtypeStruct((db, SSM_WIDTH), BF16),
                   jax.ShapeDtypeStruct((db, SSM_WIDTH, SSM_STATE), F32)),
        scratch_shapes=[pltpu.VMEM((bb, SSM_WIDTH), F32)],
        compiler_params=_cparams(("parallel",)),
        name="ssd_step",
    )(xbc, cbuf_t, z, dt, state, cw, cb, dtb, alog, dskip, gssm)


FFN_CHUNK = D_FF


def _ffn_body(*refs, tm, sample):
    if sample:
        (x_ref, oa_ref, os_ref, wo_ref, gf_ref, wu_ref, cw_ref, cb_ref, wd_ref, gfin_ref,
         p2_ref, p1_ref, y_ref, up_ref) = refs
    else:
        (x_ref, oa_ref, os_ref, wo_ref, gf_ref, wu_ref, cw_ref, cb_ref, wd_ref, gfin_ref,
         y_ref, carry_ref, *ext_refs) = refs
        pad = 8

        @pl.when(pl.program_id(1) == 0)
        def _():
            carry_ref[...] = jnp.zeros(carry_ref.shape, F32)

    x1 = (x_ref[...]
          + jnp.dot(oa_ref[...], wo_ref[0:ATTN_WIDTH, :], preferred_element_type=F32)
          + jnp.dot(os_ref[...], wo_ref[ATTN_WIDTH:, :], preferred_element_type=F32))
    r = lax.rsqrt(jnp.mean(x1 * x1, axis=-1, keepdims=True) + EPS)
    h2 = (x1 * r * gf_ref[...]).astype(BF16)
    cw = cw_ref[...]
    cbias = cb_ref[...]

    n_chunks = D_FF // FFN_CHUNK

    def col_ranges(c):
        return [(lo, lo + FFN_CHUNK) for lo in (c * FFN_CHUNK, D_FF + c * FFN_CHUNK)]

    def up_proj(c):
        return [jnp.dot(h2, wu_ref[:, lo:hi], preferred_element_type=F32) for lo, hi in col_ranges(c)]

    acc = jnp.zeros_like(x1)
    u_next = up_proj(0)
    for c in range(n_chunks):
        u_cur = u_next
        if c + 1 < n_chunks:
            u_next = up_proj(c + 1)
        parts = []
        for k, ((lo, hi), u) in enumerate(zip(col_ranges(c), u_cur)):
            if sample:
                up_ref[:, lo:hi] = u
                u1 = p1_ref[:, lo:hi]
                u2 = p2_ref[:, lo:hi]
            else:
                ext = ext_refs[2 * (c % 2) + k]
                ext[0:pad, :] = carry_ref[:, lo:hi]
                ext[pad:pad + tm, :] = u
                u1 = ext[pad - 1:pad - 1 + tm, :]
                u2 = ext[pad - 2:pad - 2 + tm, :]
                carry_ref[:, lo:hi] = ext[tm:tm + pad, :]
            parts.append(cbias[:, lo:hi] + cw[0:1, lo:hi] * u2 + cw[1:2, lo:hi] * u1 + cw[2:3, lo:hi] * u)
        gate, val = parts
        act = ((gate * _sigmoid(gate)) * val).astype(BF16)
        acc = acc + jnp.dot(act, wd_ref[c * FFN_CHUNK:(c + 1) * FFN_CHUNK, :], preferred_element_type=F32)
    x2 = x1 + acc
    r2 = lax.rsqrt(jnp.mean(x2 * x2, axis=-1, keepdims=True) + EPS)
    y_ref[...] = x2 * r2 * gfin_ref[...]


def _ffn_weights_specs():
    return [_const_spec((D_MODEL, D_MODEL)), _const_spec((1, D_MODEL)),
            _const_spec((D_MODEL, 2 * D_FF)), _const_spec((FFN_CONV, 2 * D_FF)),
            _const_spec((1, 2 * D_FF)), _const_spec((D_FF, D_MODEL)), _const_spec((1, D_MODEL))]


def _ffn_prompt(x2d, oa, os_, weights, bsz, seq, tm):
    nt = seq // tm
    row = lambda w: pl.BlockSpec((tm, w), lambda b, i: (b * nt + i, 0))
    return pl.pallas_call(
        functools.partial(_ffn_body, tm=tm, sample=False),
        grid=(bsz, nt),
        in_specs=[row(D_MODEL), row(ATTN_WIDTH), row(SSM_WIDTH)] + _ffn_weights_specs(),
        out_specs=(row(D_MODEL), pl.BlockSpec((None, 8, 2 * D_FF), lambda b, i: (b, 0, 0))),
        out_shape=(jax.ShapeDtypeStruct((bsz * seq, D_MODEL), F32),
                   jax.ShapeDtypeStruct((bsz, 8, 2 * D_FF), F32)),
        scratch_shapes=[pltpu.VMEM((tm + 8, FFN_CHUNK), F32)] * (2 * min(2, D_FF // FFN_CHUNK)),
        compiler_params=_cparams(("parallel", "arbitrary")),
        name="ffn_prompt",
    )(x2d, oa, os_, *weights)


def _ffn_sample(x2d, oa, os_, weights, prev2, prev1):
    db = x2d.shape[0]
    full = lambda w: pl.BlockSpec((db, w), lambda i: (0, 0))
    return pl.pallas_call(
        functools.partial(_ffn_body, tm=db, sample=True),
        grid=(1,),
        in_specs=[full(D_MODEL), full(ATTN_WIDTH), full(SSM_WIDTH)] + _ffn_weights_specs()
                 + [full(2 * D_FF), full(2 * D_FF)],
        out_specs=(full(D_MODEL), full(2 * D_FF)),
        out_shape=(jax.ShapeDtypeStruct((db, D_MODEL), F32),
                   jax.ShapeDtypeStruct((db, 2 * D_FF), F32)),
        compiler_params=_cparams(("arbitrary",)),
        name="ffn_sample",
    )(x2d, oa, os_, *weights, prev2, prev1)


def _pick(pref, n):
    t = min(pref, n)
    assert n % t == 0, (pref, n)
    return t


def kernel(x_prompt, x_sample, cache_k, cache_v, page_table, state_ssm, state_ssm_conv,
           state_ffn_conv, g_mix, w_in, lam_q1, lam_k1, lam_q2, lam_k2, g_subln,
           ssm_conv_w, ssm_conv_b, dt_bias, A_log, D_skip, g_ssm, w_out, g_ffn, w_up,
           ffn_conv_w, ffn_conv_b, w_down, g_final):
    bsz, seq, _ = x_prompt.shape
    db = x_sample.shape[0]
    assert x_sample.shape[1] == 1 and w_in.shape[0] == 1

    in_cols = w_in.shape[-1]
    w_in_b = jnp.pad(w_in[0], ((0, 0), (0, IN_COLS_PAD - in_cols))).astype(BF16)
    lam4 = jnp.concatenate([lam_q1, lam_k1, lam_q2, lam_k2], axis=0)
    head_pad = lambda v: jnp.pad(v, ((0, 0), (0, DT_PAD - N_SSM_HEADS)))
    dtb, alog = head_pad(dt_bias), head_pad(A_log)
    dskip = jnp.repeat(D_skip, SSM_HEAD_DIM, axis=-1)
    ssd_params = (ssm_conv_w[0], ssm_conv_b, dtb, alog, dskip, g_ssm)
    ffn_weights = (w_out[0].astype(BF16), g_ffn, w_up[0].astype(BF16), ffn_conv_w[0], ffn_conv_b,
                   w_down[0].astype(BF16), g_final.reshape(1, D_MODEL))

    n = bsz * seq
    xp = x_prompt.reshape(n, D_MODEL)
    tm_p = _pick(512, seq)
    q, k, v, kb, vbt, z, xbc, dt = _in_proj(xp, g_mix, w_in_b, tm_p)
    o_attn = _attn_prompt(q, kb, vbt, lam4, g_subln, bsz, seq, _pick(1024, seq), tm_p)
    o_ssm, st_p = _ssd_prompt(xbc, z, dt, *ssd_params, bsz, seq, _pick(512, seq))
    y_p, ffn_carry = _ffn_prompt(xp, o_attn, o_ssm, ffn_weights, bsz, seq, _pick(512, seq))

    kv_shape = (1, bsz, seq, N_ATTN_HEADS, DA_V_DIM)
    outs_p = (y_p.reshape(bsz, seq, D_MODEL), k.reshape(kv_shape), v.reshape(kv_shape),
              st_p.reshape(1, bsz, N_SSM_HEADS, SSM_HEAD_DIM, SSM_STATE),
              xbc.reshape(bsz, seq, CONV_CH)[None, :, seq - (SSM_CONV - 1):],
              ffn_carry[None, :, 8 - (FFN_CONV - 1):])

    xs = x_sample.reshape(db, D_MODEL)
    q_s, k_s, v_s, _, _, z_s, xbc_s, dt_s = _in_proj(xs, g_mix, w_in_b, db)
    n_pages = page_table.shape[1]
    o_attn_s = _attn_paged(page_table, lam4, g_subln, q_s, k_s, v_s, cache_k, cache_v,
                           _pick(16, n_pages))
    cbuf_t = jnp.swapaxes(state_ssm_conv[0], 0, 1)
    o_ssm_s, st_s = _ssd_step(xbc_s, cbuf_t, z_s, dt_s,
                              state_ssm[0].reshape(db, SSM_WIDTH, SSM_STATE), *ssd_params, _pick(8, db))
    y_s, up_s = _ffn_sample(xs, o_attn_s, o_ssm_s, ffn_weights,
                            state_ffn_conv[0, :, 0], state_ffn_conv[0, :, 1])

    kv_shape_s = (1, db, 1, N_ATTN_HEADS, DA_V_DIM)
    outs_s = (y_s.reshape(db, 1, D_MODEL), k_s.reshape(kv_shape_s), v_s.reshape(kv_shape_s),
              st_s.reshape(1, db, N_SSM_HEADS, SSM_HEAD_DIM, SSM_STATE),
              jnp.concatenate([state_ssm_conv[:, :, 1:], xbc_s[None, :, None, :]], axis=2),
              jnp.concatenate([state_ffn_conv[:, :, 1:], up_s[None, :, None, :]], axis=2))

    return (outs_p[0], outs_s[0]) + outs_p[1:] + outs_s[1:]
```
